```python
import jax
import jax.numpy as jnp
from jax import lax
import numpy as np

D_MODEL = 1024
BATCH = 8
SEQ = 4096
DEPTH = 1

HEAD_DIM = 64
RWKV_HEADS = 8
RWKV_WIDTH = RWKV_HEADS * HEAD_DIM
MOBA_HEADS = 8
MOBA_WIDTH = MOBA_HEADS * HEAD_DIM
DECAY_LORA = 64
AAA_LORA = 64
GATE_LORA = 160
MOBA_BLOCK = 256
MOBA_TOPK = 3
Q_CHUNK = 64
ROT_DIM = HEAD_DIM // 4
ROPE_THETA = 500000.0
D_FF = 2816
CONV_WIDTH = 3
NORM_EPS = 1e-6
GN_EPS = 64e-5
N_BRANCH = 2
RWKV_IN = 3 * RWKV_WIDTH + DECAY_LORA + AAA_LORA + GATE_LORA
MOBA_IN = 3 * MOBA_WIDTH
GATE_IN = N_BRANCH * D_MODEL
IN_COLS = RWKV_IN + MOBA_IN + GATE_IN

kernel_name = 'hybrid_rwkv7_moba_convffn_block'


def rms_norm(x, g):
    xf = x.astype(jnp.float32)
    y = xf * lax.rsqrt(jnp.mean(xf * xf, axis=-1, keepdims=True) + NORM_EPS)
    return (y * g.astype(jnp.float32)).astype(x.dtype)


def shift_right(z):
    return jnp.pad(z, ((0, 0), (1, 0), (0, 0)))[:, :-1]


def partial_rope(x, positions):
    half = ROT_DIM // 2
    inv_freq = ROPE_THETA ** (-jnp.arange(half, dtype=jnp.float32) / half)
    ang = positions.astype(jnp.float32)[..., None] * inv_freq
    cos = jnp.cos(ang)[:, :, None, :]
    sin = jnp.sin(ang)[:, :, None, :]
    xr = x[..., :ROT_DIM].astype(jnp.float32)
    x1, x2 = xr[..., :half], xr[..., half:]
    rot = jnp.concatenate([x1 * cos - x2 * sin, x2 * cos + x1 * sin], axis=-1)
    return jnp.concatenate([rot.astype(x.dtype), x[..., ROT_DIM:]], axis=-1)


def rwkv7_time_mix(z, w_decay_up, decay_bias, w_aaa_up, aaa_bias, w_gate_up,
                   k_k, k_a, r_k, ln_g, ln_b):
    B, S, _ = z.shape
    H, N, C = RWKV_HEADS, HEAD_DIM, RWKV_WIDTH
    f32 = jnp.float32
    o3 = 3 * C
    o4 = o3 + DECAY_LORA
    o5 = o4 + AAA_LORA
    r, k, v = z[..., :C], z[..., C:2 * C], z[..., 2 * C:o3]
    xw, xa, xg = z[..., o3:o4], z[..., o4:o5], z[..., o5:]
    w_log = -jax.nn.softplus(-(decay_bias + jnp.tanh(xw) @ w_decay_up)) - 0.5
    decay = jnp.exp(-jnp.exp(w_log.astype(f32)))
    a = jax.nn.sigmoid(aaa_bias + xa @ w_aaa_up)
    g = jax.nn.sigmoid(xg) @ w_gate_up
    kk = (k * k_k).astype(f32).reshape(B, S, H, N)
    kk = kk / jnp.maximum(jnp.sqrt(jnp.sum(kk * kk, axis=-1, keepdims=True)), 1e-12)
    k = k * (1.0 + (a - 1.0) * k_a)

    def time_major(t):
        return t.astype(f32).reshape(B, S, H, N).transpose(1, 0, 2, 3)

    kk_t = kk.transpose(1, 0, 2, 3)
    inputs = (time_major(r), time_major(decay), time_major(k), time_major(v),
              -kk_t, kk_t * time_major(a))

    def step(state, inp):
        r_t, w_t, k_t, v_t, a_t, b_t = inp
        sa = jnp.einsum('bhvk,bhk->bhv', state, a_t)
        state = (state * w_t[:, :, None, :] + sa[..., None] * b_t[:, :, None, :]
                 + v_t[..., None] * k_t[:, :, None, :])
        return state, jnp.einsum('bhvk,bhk->bhv', state, r_t)

    state0 = jnp.zeros((B, H, N, N), f32)
    _, y = lax.scan(step, state0, inputs)
    y = y.transpose(1, 0, 2, 3)
    mu = jnp.mean(y, axis=-1, keepdims=True)
    var = jnp.mean(jnp.square(y - mu), axis=-1, keepdims=True)
    y = ((y - mu) * lax.rsqrt(var + GN_EPS)).reshape(B, S, C)
    y = y * ln_g.astype(f32) + ln_b.astype(f32)
    rh = r.astype(f32).reshape(B, S, H, N)
    kh = k.astype(f32).reshape(B, S, H, N)
    vh = v.astype(f32).reshape(B, S, H, N)
    bonus = (jnp.sum(rh * kh * r_k.astype(f32), axis=-1, keepdims=True) * vh).reshape(B, S, C)
    return ((y + bonus) * g.astype(f32)).astype(z.dtype)


def moba_attention(q, k, v):
    B, S, H, Dh = q.shape
    f32 = jnp.float32
    n_blk = -(-S // MOBA_BLOCK)
    S_pad = n_blk * MOBA_BLOCK
    pad = ((0, 0), (0, S_pad - S), (0, 0), (0, 0))
    qt = jnp.pad(q, pad).transpose(0, 2, 1, 3)
    kt = jnp.pad(k, pad).transpose(0, 2, 1, 3)
    vt = jnp.pad(v, pad).transpose(0, 2, 1, 3)
    kb = kt.reshape(B, H, n_blk, MOBA_BLOCK, Dh)
    vb = vt.reshape(B, H, n_blk, MOBA_BLOCK, Dh)
    k_mean = jnp.mean(kb.astype(f32), axis=3)
    gate = jnp.einsum('bhsd,bhnd->bhsn', qt.astype(f32), k_mean)
    q_blk = jnp.arange(S_pad) // MOBA_BLOCK
    past = jnp.arange(n_blk)[None, :] < q_blk[:, None]
    gate = jnp.where(past, gate, -jnp.inf)
    n_sel = min(MOBA_TOPK, n_blk)
    _, sel = lax.top_k(gate, n_sel)
    slot_ok = jnp.arange(n_sel)[None, :] < q_blk[:, None]
    b_idx = jnp.arange(B)[:, None, None, None]
    h_idx = jnp.arange(H)[None, :, None, None]
    scale = Dh ** -0.5
    q_off = jnp.arange(Q_CHUNK)
    k_off = jnp.arange(MOBA_BLOCK)
    n_keys_sel = n_sel * MOBA_BLOCK

    def attend_chunk(c):
        start = c * Q_CHUNK
        blk = start // MOBA_BLOCK
        q_c = lax.dynamic_slice_in_dim(qt, start, Q_CHUNK, axis=2)
        sel_c = lax.dynamic_slice_in_dim(sel, start, Q_CHUNK, axis=2)
        ok_c = lax.dynamic_slice_in_dim(slot_ok, start, Q_CHUNK, axis=0)
        k_sel = kb[b_idx, h_idx, sel_c]
        v_sel = vb[b_idx, h_idx, sel_c]
        k_own = lax.dynamic_index_in_dim(kb, blk, axis=2, keepdims=False)
        v_own = lax.dynamic_index_in_dim(vb, blk, axis=2, keepdims=False)
        s_sel = jnp.einsum('bhqd,bhqnkd->bhqnk', q_c, k_sel, preferred_element_type=f32) * scale
        s_sel = jnp.where(ok_c[None, None, :, :, None], s_sel, -jnp.inf)
        s_own = jnp.einsum('bhqd,bhkd->bhqk', q_c, k_own, preferred_element_type=f32) * scale
        causal = (blk * MOBA_BLOCK + k_off)[None, :] <= (start + q_off)[:, None]
        s_own = jnp.where(causal, s_own, -jnp.inf)
        s_all = jnp.concatenate([s_sel.reshape(B, H, Q_CHUNK, n_keys_sel), s_own], axis=-1)
        p = jax.nn.softmax(s_all, axis=-1).astype(v.dtype)
        p_sel = p[..., :n_keys_sel].reshape(B, H, Q_CHUNK, n_sel, MOBA_BLOCK)
        p_own = p[..., n_keys_sel:]
        o = (jnp.einsum('bhqnk,bhqnkd->bhqd', p_sel, v_sel, preferred_element_type=f32)
             + jnp.einsum('bhqk,bhkd->bhqd', p_own, v_own, preferred_element_type=f32))
        return o.astype(v.dtype)

    out = lax.map(attend_chunk, jnp.arange(S_pad // Q_CHUNK))
    out = out.transpose(1, 0, 3, 2, 4).reshape(B, S_pad, H * Dh)
    return out[:, :S]


def conv_ffn(h, w_up, conv_w, conv_b, w_down):
    u = h @ w_up
    a, b = u[..., :D_FF], u[..., D_FF:]
    a = lax.conv_general_dilated(
        a, conv_w.reshape(CONV_WIDTH, 1, D_FF), window_strides=(1,),
        padding=[(CONV_WIDTH - 1, 0)], dimension_numbers=('NWC', 'WIO', 'NWC'),
        feature_group_count=D_FF) + conv_b
    return (jax.nn.gelu(a, approximate=False) * b) @ w_down


def setup_inputs(seed: int = 0) -> dict:
    key = jax.random.key(seed)
    ks = iter(jax.random.split(key, 32))
    L = DEPTH
    f32 = jnp.float32

    def nrm(shape, scale):
        return jax.random.normal(next(ks), shape, f32) * scale

    def unif(shape, lo, hi):
        return jax.random.uniform(next(ks), shape, f32, lo, hi)

    x = nrm((BATCH, SEQ, D_MODEL), 1.0)
    positions = jnp.broadcast_to(jnp.arange(SEQ, dtype=jnp.int32), (BATCH, SEQ))
    return {
        'x': x,
        'positions': positions,
        'norm1_g': 1.0 + nrm((L, D_MODEL), 0.02),
        'w_in': nrm((L, D_MODEL, IN_COLS), D_MODEL ** -0.5),
        'rwkv_mu': unif((L, RWKV_IN), 0.0, 1.0),
        'w_decay_up': nrm((L, DECAY_LORA, RWKV_WIDTH), 0.5 * DECAY_LORA ** -0.5),
        'decay_bias': unif((L, RWKV_WIDTH), -6.5, -1.5),
        'w_aaa_up': nrm((L, AAA_LORA, RWKV_WIDTH), 0.5 * AAA_LORA ** -0.5),
        'aaa_bias': nrm((L, RWKV_WIDTH), 0.1),
        'w_gate_up': nrm((L, GATE_LORA, RWKV_WIDTH), GATE_LORA ** -0.5),
        'rwkv_k_k': 0.85 + nrm((L, RWKV_WIDTH), 0.05),
        'rwkv_k_a': 1.0 + nrm((L, RWKV_WIDTH), 0.05),
        'rwkv_r_k': nrm((L, RWKV_HEADS, HEAD_DIM), 0.1),
        'rwkv_ln_g': 1.0 + nrm((L, RWKV_WIDTH), 0.02),
        'rwkv_ln_b': nrm((L, RWKV_WIDTH), 0.02),
        'q_norm_g': 1.0 + nrm((L, HEAD_DIM), 0.02),
        'k_norm_g': 1.0 + nrm((L, HEAD_DIM), 0.02),
        'w_branch_a': nrm((L, RWKV_WIDTH, D_MODEL), RWKV_WIDTH ** -0.5),
        'w_branch_b': nrm((L, MOBA_WIDTH, D_MODEL), MOBA_WIDTH ** -0.5),
        'w_out': nrm((L, D_MODEL, D_MODEL), D_MODEL ** -0.5),
        'norm2_g': 1.0 + nrm((L, D_MODEL), 0.02),
        'w_ffn_up': nrm((L, D_MODEL, 2 * D_FF), D_MODEL ** -0.5),
        'ffn_conv_w': nrm((L, CONV_WIDTH, D_FF), CONV_WIDTH ** -0.5),
        'ffn_conv_b': nrm((L, D_FF), 0.02),
        'w_ffn_down': nrm((L, D_FF, D_MODEL), D_FF ** -0.5),
    }


def reference(x, positions, norm1_g, w_in, rwkv_mu, w_decay_up, decay_bias, w_aaa_up,
              aaa_bias, w_gate_up, rwkv_k_k, rwkv_k_a, rwkv_r_k, rwkv_ln_g, rwkv_ln_b,
              q_norm_g, k_norm_g, w_branch_a, w_branch_b, w_out, norm2_g, w_ffn_up,
              ffn_conv_w, ffn_conv_b, w_ffn_down):
    B, S, _ = x.shape
    for l in range(DEPTH):
        h = rms_norm(x, norm1_g[l])
        proj = h @ w_in[l]
        z_rwkv = proj[..., :RWKV_IN]
        z_moba = proj[..., RWKV_IN:RWKV_IN + MOBA_IN]
        gate_pre = proj[..., RWKV_IN + MOBA_IN:]
        z_rwkv = z_rwkv + rwkv_mu[l] * (shift_right(z_rwkv) - z_rwkv)
        y_a = rwkv7_time_mix(z_rwkv, w_decay_up[l], decay_bias[l], w_aaa_up[l], aaa_bias[l],
                             w_gate_up[l], rwkv_k_k[l], rwkv_k_a[l], rwkv_r_k[l],
                             rwkv_ln_g[l], rwkv_ln_b[l])
        q = z_moba[..., :MOBA_WIDTH].reshape(B, S, MOBA_HEADS, HEAD_DIM)
        k = z_moba[..., MOBA_WIDTH:2 * MOBA_WIDTH].reshape(B, S, MOBA_HEADS, HEAD_DIM)
        v = z_moba[..., 2 * MOBA_WIDTH:].reshape(B, S, MOBA_HEADS, HEAD_DIM)
        q = partial_rope(rms_norm(q, q_norm_g[l]), positions)
        k = partial_rope(rms_norm(k, k_norm_g[l]), positions)
        y_b = moba_attention(q, k, v)
        u_a = y_a @ w_branch_a[l]
        u_b = y_b @ w_branch_b[l]
        g_a = jax.nn.sigmoid(gate_pre[..., :D_MODEL])
        g_b = jax.nn.sigmoid(gate_pre[..., D_MODEL:])
        x = x + (g_a * u_a + g_b * u_b) @ w_out[l]
        h2 = rms_norm(x, norm2_g[l])
        x = x + conv_ffn(h2, w_ffn_up[l], ffn_conv_w[l], ffn_conv_b[l], w_ffn_down[l])
    return x
```

```python
import functools

import numpy as np
import jax
import jax.numpy as jnp
from jax import lax
from jax.experimental import pallas as pl
from jax.experimental.pallas import tpu as pltpu

F32 = jnp.float32
BF16 = jnp.bfloat16
HI = lax.Precision.HIGHEST

D_MODEL = 1024
HEAD_DIM = 64
N_HEADS = 8
WIDTH = N_HEADS * HEAD_DIM
DECAY_LORA = 64
AAA_LORA = 64
GATE_LORA = 160
LORA_PAD = 512
MOBA_BLOCK = 256
MOBA_TOPK = 3
ROT_DIM = HEAD_DIM // 4
ROPE_THETA = 500000.0
D_FF = 2816
NORM_EPS = 1e-6
GN_EPS = 64e-5
LANES = 128
CHUNK = 64
NEG_BIG = -32768.0
VMEM_LIMIT = 56 * 1024 * 1024


def _sigmoid(x):
    return 1.0 / (1.0 + jnp.exp(-x))


def _dot(a, b, precision=None):
    return jnp.dot(a, b, preferred_element_type=F32, precision=precision)


def _dot_nt(a, b, precision=None):
    return lax.dot_general(a, b, (((1,), (1,)), ((), ())), preferred_element_type=F32,
                           precision=precision)


def _dot_tn(a, b, precision=None):
    return lax.dot_general(a, b, (((0,), (0,)), ((), ())), preferred_element_type=F32,
                           precision=precision)


def _params(*sem):
    return pltpu.CompilerParams(dimension_semantics=sem, vmem_limit_bytes=VMEM_LIMIT)


def _const_spec(shape):
    nd = len(shape)
    return pl.BlockSpec(shape, lambda *_: (0,) * nd, pipeline_mode=pl.Buffered(1))


def _inproj_kernel(x_ref, g_ref, wr_ref, wl_ref, wm_ref, wg_ref,
                   zr_ref, zl_ref, zm_ref, gate_ref):
    x = x_ref[...]
    ms = jnp.mean(x * x, axis=-1, keepdims=True)
    h = (x * lax.rsqrt(ms + NORM_EPS) * g_ref[...]).astype(BF16)
    zr_ref[...] = _dot(h, wr_ref[...])
    zl_ref[...] = _dot(h, wl_ref[...])
    zm_ref[...] = _dot(h, wm_ref[...])
    gate_ref[...] = _sigmoid(_dot(h, wg_ref[...])).astype(BF16)


def _inproj(x2, g, wr, wl, wm, wg, tm):
    t = x2.shape[0]
    row = lambda n: pl.BlockSpec((tm, n), lambda i: (i, 0))
    return pl.pallas_call(
        _inproj_kernel,
        grid=(t // tm,),
        in_specs=[row(D_MODEL), _const_spec(g.shape), _const_spec(wr.shape), _const_spec(wl.shape),
                  _const_spec(wm.shape), _const_spec(wg.shape)],
        out_specs=[row(3 * WIDTH), row(LORA_PAD), row(3 * WIDTH), row(2 * D_MODEL)],
        out_shape=[jax.ShapeDtypeStruct((t, 3 * WIDTH), F32),
                   jax.ShapeDtypeStruct((t, LORA_PAD), F32),
                   jax.ShapeDtypeStruct((t, 3 * WIDTH), F32),
                   jax.ShapeDtypeStruct((t, 2 * D_MODEL), BF16)],
        compiler_params=_params("parallel"),
        name="inproj",
    )(x2, g, wr, wl, wm, wg)


def _rwkv_kernel(zr_ref, zrp_ref, zl_ref, zlp_ref, mur_ref, mul_ref, wd_ref, db_ref, wa_ref,
                 ab_ref, wg_ref, kk_ref, ka_ref, rk_ref, lng_ref, lnb_ref, o_ref, state_ref):
    c = pl.program_id(1)
    L = zr_ref.shape[0]
    N = HEAD_DIM

    @pl.when(c == 0)
    def _():
        state_ref[...] = jnp.zeros_like(state_ref)

    def token_shift(z_ref, zp_ref, mu_ref):
        z = z_ref[...]
        prev = jnp.where(c == 0, 0.0, zp_ref[7:8, :])
        row = lax.broadcasted_iota(jnp.int32, z.shape, 0)
        zs = jnp.where(row == 0, prev, pltpu.roll(z, 1, axis=0))
        return z + mu_ref[...] * (zs - z)

    zr = token_shift(zr_ref, zrp_ref, mur_ref)
    zl = token_shift(zl_ref, zlp_ref, mul_ref)
    r = zr[:, 0:WIDTH]
    k = zr[:, WIDTH:2 * WIDTH]
    v = zr[:, 2 * WIDTH:3 * WIDTH]
    xw = zl[:, 0:LANES]
    xa = zl[:, LANES:2 * LANES]
    xg = zl[:, 2 * LANES:4 * LANES]

    dd = db_ref[...] + _dot(jnp.tanh(xw), wd_ref[...], HI)
    w_log = -(jnp.maximum(-dd, 0.0) + jnp.log(1.0 + jnp.exp(-jnp.abs(dd)))) - 0.5
    lw = -jnp.exp(w_log)
    asig = _sigmoid(ab_ref[...] + _dot(xa, wa_ref[...], HI))
    g = _dot(_sigmoid(xg), wg_ref[...], HI)
    kkf = k * kk_ref[...]
    kmod = k * (1.0 + (asig - 1.0) * ka_ref[...])
    rkk = r * kmod * rk_ref[...]

    rowL = lax.broadcasted_iota(jnp.int32, (L, L), 0)
    colL = lax.broadcasted_iota(jnp.int32, (L, L), 1)
    tri_incl = (rowL >= colL).astype(F32)
    lower = rowL > colL
    lower_incl = rowL >= colL
    eye = (rowL == colL).astype(F32)
    cw = _dot(tri_incl, lw, HI)

    for h in range(N_HEADS):
        sl = slice(h * N, (h + 1) * N)
        cw_h = cw[:, sl]
        lw_h = lw[:, sl]
        r_h, k_h, v_h = r[:, sl], kmod[:, sl], v[:, sl]
        kk_h = kkf[:, sl]
        kk_h = kk_h / jnp.maximum(jnp.sqrt(jnp.sum(kk_h * kk_h, axis=-1, keepdims=True)), 1e-12)
        b_h = kk_h * asig[:, sl]
        cw_end = cw_h[L - 1:L, :]
        e_neg = jnp.exp(-cw_h)
        e_end = jnp.exp(cw_end - cw_h)
        a_til = -kk_h * jnp.exp(cw_h - lw_h)
        r_til = r_h * jnp.exp(cw_h)
        b_til = b_h * e_neg
        k_til = k_h * e_neg
        b_hat = b_h * e_end
        k_hat = k_h * e_end

        a_ab = jnp.where(lower, _dot_nt(a_til, b_til, HI), 0.0)
        a_ak = jnp.where(lower, _dot_nt(a_til, k_til, HI), 0.0)
        a_rb = jnp.where(lower_incl, _dot_nt(r_til, b_til, HI), 0.0)
        a_rk = jnp.where(lower_incl, _dot_nt(r_til, k_til, HI), 0.0)

        tinv = eye + a_ab
        pw = a_ab
        for _ in range(int(np.log2(L)) - 1):
            pw = _dot(pw, pw, HI)
            tinv = tinv + _dot(tinv, pw, HI)

        st = state_ref[h]
        u = _dot(tinv, _dot(a_til, st, HI) + _dot(a_ak, v_h, HI), HI)
        y = _dot(r_til, st, HI) + _dot(a_rb, u, HI) + _dot(a_rk, v_h, HI)
        w_col = jnp.broadcast_to(jnp.exp(cw_end), (N, N)).T
        state_ref[h] = w_col * st + _dot_tn(b_hat, u, HI) + _dot_tn(k_hat, v_h, HI)

        mu = jnp.mean(y, axis=-1, keepdims=True)
        var = jnp.mean(jnp.square(y - mu), axis=-1, keepdims=True)
        yn = (y - mu) * lax.rsqrt(var + GN_EPS) * lng_ref[:, sl] + lnb_ref[:, sl]
        bonus = jnp.sum(rkk[:, sl], axis=-1, keepdims=True) * v_h
        o_ref[:, sl] = ((yn + bonus) * g[:, sl]).astype(o_ref.dtype)


def _rwkv(zr, zl, consts, batch, seq):
    L = CHUNK
    nc = seq // L
    cur = lambda n: pl.BlockSpec((None, L, n), lambda b, c: (b, c, 0))
    prv = lambda n: pl.BlockSpec((None, 8, n), lambda b, c: (b, jnp.maximum(c * (L // 8) - 1, 0), 0))
    zr3 = zr.reshape(batch, seq, 3 * WIDTH)
    zl3 = zl.reshape(batch, seq, LORA_PAD)
    return pl.pallas_call(
        _rwkv_kernel,
        grid=(batch, nc),
        in_specs=[cur(3 * WIDTH), prv(3 * WIDTH), cur(LORA_PAD), prv(LORA_PAD)]
                 + [_const_spec(a.shape) for a in consts],
        out_specs=pl.BlockSpec((None, L, WIDTH), lambda b, c: (b, c, 0)),
        out_shape=jax.ShapeDtypeStruct((batch, seq, WIDTH), BF16),
        scratch_shapes=[pltpu.VMEM((N_HEADS, HEAD_DIM, HEAD_DIM), F32)],
        compiler_params=_params("parallel", "arbitrary"),
        name="rwkv",
    )(zr3, zr3, zl3, zl3, *consts)


def _moba_prep_kernel(zm_ref, pos_ref, invf_ref, qg_ref, kg_ref, mavg_ref,
                      q_out, k_out, v_out, kmean_ref, *, n_blocks):
    i = pl.program_id(1)
    TB = zm_ref.shape[0]

    @pl.when(i == 0)
    def _():
        kmean_ref[...] = jnp.zeros_like(kmean_ref)

    zm = zm_ref[...]
    ang = pos_ref[...].astype(F32) * invf_ref[...]
    lane = lax.broadcasted_iota(jnp.int32, (TB, LANES), 1)
    lane_h = lane % HEAD_DIM
    cos = jnp.cos(ang)
    sin = jnp.sin(ang)
    half = ROT_DIM // 2
    sin_lo = jnp.where(lane_h < half, -sin, 0.0)
    sin_hi = jnp.where(lane_h >= half, sin, 0.0)
    rep = WIDTH // LANES
    cos = jnp.concatenate([cos] * rep, axis=-1)
    sin_lo = jnp.concatenate([sin_lo] * rep, axis=-1)
    sin_hi = jnp.concatenate([sin_hi] * rep, axis=-1)

    def norm_rope(t, gain):
        ms = _dot(t * t, mavg_ref[...], HI)
        t = t * lax.rsqrt(ms + NORM_EPS) * gain
        return (t * cos + pltpu.roll(t, WIDTH - half, axis=1) * sin_lo
                + pltpu.roll(t, half, axis=1) * sin_hi)

    q = norm_rope(zm[:, 0:WIDTH], qg_ref[...])
    k = norm_rope(zm[:, WIDTH:2 * WIDTH], kg_ref[...])
    v = zm[:, 2 * WIDTH:3 * WIDTH]

    kmean = kmean_ref[...]
    n_of_lane = lane - HEAD_DIM
    aux = (lane >= HEAD_DIM) & (n_of_lane < n_blocks)
    is_data = lane < HEAD_DIM
    km_data = lax.broadcasted_iota(jnp.int32, (LANES, LANES), 1) < HEAD_DIM

    for h in range(N_HEADS):
        ps = slice((h // 2) * LANES, (h // 2 + 1) * LANES)

        def head_base(t):
            p = t[:, ps]
            return p if h % 2 == 0 else pltpu.roll(p, HEAD_DIM, axis=1)

        qb, kb, vb = head_base(q), head_base(k), head_base(v)
        km = head_base(kmean)
        gate = _dot_nt(jnp.where(is_data, qb, 0.0), jnp.where(km_data, km, 0.0), HI)
        gsel = jnp.where(aux & (n_of_lane < i), gate, -jnp.inf)
        picked = lane < 0
        for _ in range(MOBA_TOPK):
            m = jnp.max(gsel, axis=-1, keepdims=True)
            idx = jnp.min(jnp.where(gsel == m, lane, 2 * LANES), axis=-1, keepdims=True)
            pick = lane == idx
            picked = picked | (pick & (m > -jnp.inf))
            gsel = jnp.where(pick, -jnp.inf, gsel)
        keep = picked | (n_of_lane == i)
        bias = jnp.where(aux & jnp.logical_not(keep), NEG_BIG, 0.0)
        q_out[h] = jnp.where(is_data, qb * (HEAD_DIM ** -0.5), bias).astype(BF16)
        k_out[h] = jnp.where(is_data, kb, jnp.where(n_of_lane == i, 1.0, 0.0)).astype(BF16)
        v_out[h] = jnp.where(is_data, vb, jnp.where(lane == HEAD_DIM, 1.0, 0.0)).astype(BF16)

    rowk = lax.broadcasted_iota(jnp.int32, kmean.shape, 0)
    kmean_ref[...] = jnp.where(rowk == HEAD_DIM + i, jnp.mean(k, axis=0, keepdims=True), kmean)


def _moba_prep(zm, pos3, invf, qg, kg, mavg, batch, seq):
    TB = MOBA_BLOCK
    nb = seq // TB
    zm3 = zm.reshape(batch, seq, 3 * WIDTH)
    aug = pl.BlockSpec((None, N_HEADS, TB, LANES), lambda b, i: (b, 0, i, 0))
    aug_shape = jax.ShapeDtypeStruct((batch, N_HEADS, seq, LANES), BF16)
    return pl.pallas_call(
        functools.partial(_moba_prep_kernel, n_blocks=nb),
        grid=(batch, nb),
        in_specs=[pl.BlockSpec((None, TB, 3 * WIDTH), lambda b, i: (b, i, 0)),
                  pl.BlockSpec((None, TB, 1), lambda b, i: (b, i, 0)),
                  _const_spec(invf.shape), _const_spec(qg.shape), _const_spec(kg.shape),
                  _const_spec(mavg.shape)],
        out_specs=[aug, aug, aug],
        out_shape=[aug_shape, aug_shape, aug_shape],
        scratch_shapes=[pltpu.VMEM((LANES, WIDTH), F32)],
        compiler_params=_params("parallel", "arbitrary"),
        name="moba_prep",
    )(zm3, pos3, invf, qg, kg, mavg)


def _moba_attn_kernel(q_ref, k_ref, v_ref, o_ref, s_ref, mx_ref):
    i = pl.program_id(2)
    TB = q_ref.shape[1]
    HP = q_ref.shape[0]
    row = lax.broadcasted_iota(jnp.int32, (TB, TB), 0)
    col = lax.broadcasted_iota(jnp.int32, (TB, TB), 1)
    causal = col <= row
    own = pl.multiple_of(i * TB, TB)

    outs = []
    for hh in range(HP):
        q = q_ref[hh]
        s_own = jnp.where(causal, _dot_nt(q, k_ref[hh, pl.ds(own, TB), :]), -1e30)
        mx_ref[hh] = s_own

        def scores(j, carry):
            s = _dot_nt(q, k_ref[hh, pl.ds(pl.multiple_of(j * TB, TB), TB), :])
            s_ref[hh, j] = s
            mx_ref[hh] = jnp.maximum(mx_ref[hh], s)
            return carry

        lax.fori_loop(0, i, scores, 0)
        m = jnp.max(mx_ref[hh], axis=-1, keepdims=True)

        def weighted(j, acc):
            p = jnp.exp(s_ref[hh, j] - m).astype(BF16)
            return acc + _dot(p, v_ref[hh, pl.ds(pl.multiple_of(j * TB, TB), TB), :])

        acc = _dot(jnp.exp(s_own - m).astype(BF16), v_ref[hh, pl.ds(own, TB), :])
        acc = lax.fori_loop(0, i, weighted, acc)
        outs.append(acc / acc[:, HEAD_DIM:HEAD_DIM + 1])

    lane = lax.broadcasted_iota(jnp.int32, (TB, LANES), 1)
    o_ref[...] = jnp.where(lane < HEAD_DIM, outs[0],
                           pltpu.roll(outs[1], HEAD_DIM, axis=1)).astype(o_ref.dtype)


def _moba_attn(qa, ka, va, batch, seq):
    TB = MOBA_BLOCK
    nb = seq // TB
    HP = 2
    return pl.pallas_call(
        _moba_attn_kernel,
        grid=(batch, N_HEADS // HP, nb),
        in_specs=[pl.BlockSpec((None, HP, TB, LANES), lambda b, p, i: (b, p, i, 0)),
                  pl.BlockSpec((None, HP, seq, LANES), lambda b, p, i: (b, p, 0, 0)),
                  pl.BlockSpec((None, HP, seq, LANES), lambda b, p, i: (b, p, 0, 0))],
        out_specs=pl.BlockSpec((None, TB, LANES), lambda b, p, i: (b, i, p)),
        out_shape=jax.ShapeDtypeStruct((batch, seq, WIDTH), BF16),
        scratch_shapes=[pltpu.VMEM((HP, nb, TB, TB), F32), pltpu.VMEM((HP, TB, TB), F32)],
        compiler_params=_params("parallel", "parallel", "arbitrary"),
        name="moba_attn",
    )(qa, ka, va)


def _merge_kernel(x_ref, ya_ref, yb_ref, gate_ref, wa_ref, wb_ref, wo_ref, g2_ref, x1_ref, h2_ref):
    ua = _dot(ya_ref[...], wa_ref[...])
    ub = _dot(yb_ref[...], wb_ref[...])
    gate = gate_ref[...].astype(F32)
    mix = (gate[:, 0:D_MODEL] * ua + gate[:, D_MODEL:] * ub).astype(BF16)
    x1 = x_ref[...] + _dot(mix, wo_ref[...])
    x1_ref[...] = x1
    ms = jnp.mean(x1 * x1, axis=-1, keepdims=True)
    h2_ref[...] = (x1 * lax.rsqrt(ms + NORM_EPS) * g2_ref[...]).astype(BF16)


def _merge(x2, ya, yb, gates, wa, wb, wo, g2, tm):
    t = x2.shape[0]
    row = lambda n: pl.BlockSpec((tm, n), lambda i: (i, 0))
    return pl.pallas_call(
        _merge_kernel,
        grid=(t // tm,),
        in_specs=[row(D_MODEL), row(WIDTH), row(WIDTH), row(2 * D_MODEL), _const_spec(wa.shape),
                  _const_spec(wb.shape), _const_spec(wo.shape), _const_spec(g2.shape)],
        out_specs=[row(D_MODEL), row(D_MODEL)],
        out_shape=[jax.ShapeDtypeStruct((t, D_MODEL), F32), jax.ShapeDtypeStruct((t, D_MODEL), BF16)],
        compiler_params=_params("parallel"),
        name="merge",
    )(x2, ya, yb, gates, wa, wb, wo, g2)


FFN_HALO = 16
FFN_SPLIT = 2


def _ffn_kernel(h_ref, hp_ref, x1_ref, wua_ref, wub_ref, cw_ref, cb_ref, wd_ref, o_ref, *, tiles_per_seq):
    i = pl.program_id(0)
    tm = h_ref.shape[0]
    h = h_ref[...]
    halo = jnp.where(i % tiles_per_seq == 0, jnp.zeros_like(hp_ref[...]), hp_ref[...])
    h_ext = jnp.concatenate([halo, h], axis=0)
    fc = D_FF // FFN_SPLIT
    acc = x1_ref[...]
    for s in range(FFN_SPLIT):
        cs = slice(s * fc, (s + 1) * fc)
        a = _dot(h_ext, wua_ref[:, cs])
        b = _dot(h, wub_ref[:, cs])
        conv = (a[FFN_HALO:, :] * cw_ref[2:3, cs] + a[FFN_HALO - 1:FFN_HALO - 1 + tm, :] * cw_ref[1:2, cs]
                + a[FFN_HALO - 2:FFN_HALO - 2 + tm, :] * cw_ref[0:1, cs] + cb_ref[:, cs])
        gelu = 0.5 * conv * (1.0 + lax.erf(conv * (2.0 ** -0.5)))
        acc = acc + _dot((gelu * b).astype(BF16), wd_ref[cs, :])
    o_ref[...] = acc


def _ffn(h2, x1, wua, wub, cw, cb, wd, tm, seq):
    t = h2.shape[0]
    row = lambda n: pl.BlockSpec((tm, n), lambda i: (i, 0))
    halo = pl.BlockSpec((FFN_HALO, D_MODEL), lambda i: (jnp.maximum(i * (tm // FFN_HALO) - 1, 0), 0))
    return pl.pallas_call(
        functools.partial(_ffn_kernel, tiles_per_seq=seq // tm),
        grid=(t // tm,),
        in_specs=[row(D_MODEL), halo, row(D_MODEL), _const_spec(wua.shape), _const_spec(wub.shape),
                  _const_spec(cw.shape), _const_spec(cb.shape), _const_spec(wd.shape)],
        out_specs=row(D_MODEL),
        out_shape=jax.ShapeDtypeStruct((t, D_MODEL), F32),
        compiler_params=_params("parallel"),
        name="ffn",
    )(h2, h2, x1, wua, wub, cw, cb, wd)


def _pad_lora_cols(a):
    z = lambda n: jnp.zeros(a.shape[:-1] + (n,), a.dtype)
    o1, o2 = DECAY_LORA, DECAY_LORA + AAA_LORA
    return jnp.concatenate([a[..., :o1], z(LANES - DECAY_LORA), a[..., o1:o2], z(LANES - AAA_LORA),
                            a[..., o2:], z(2 * LANES - GATE_LORA)], axis=-1)


def _pad_rows(a, n):
    return jnp.concatenate([a, jnp.zeros((n - a.shape[0],) + a.shape[1:], a.dtype)], axis=0)


def _rope_inv_freq():
    half = ROT_DIM // 2
    lane = np.arange(LANES) % HEAD_DIM
    f = ROPE_THETA ** (-(lane % half).astype(np.float64) / half)
    return jnp.asarray(np.where(lane < ROT_DIM, f, 0.0)[None, :], F32)


def _head_mean_matrix():
    return jnp.asarray(np.kron(np.eye(N_HEADS), np.full((HEAD_DIM, HEAD_DIM), 1.0 / HEAD_DIM)), F32)


def kernel(x, positions, norm1_g, w_in, rwkv_mu, w_decay_up, decay_bias, w_aaa_up, aaa_bias, w_gate_up, rwkv_k_k, rwkv_k_a, rwkv_r_k, rwkv_ln_g, rwkv_ln_b, q_norm_g, k_norm_g, w_branch_a, w_branch_b, w_out, norm2_g, w_ffn_up, ffn_conv_w, ffn_conv_b, w_ffn_down):
    batch, seq, _ = x.shape
    depth = norm1_g.shape[0]
    assert seq % MOBA_BLOCK == 0 and seq % 512 == 0
    t = batch * seq
    row = lambda a: a.reshape(1, -1)
    c3 = 3 * WIDTH
    rwkv_in = c3 + DECAY_LORA + AAA_LORA + GATE_LORA
    pos3 = positions.reshape(batch, seq, 1)
    invf = _rope_inv_freq()
    mavg = _head_mean_matrix()

    x2 = x.reshape(t, D_MODEL)
    for l in range(depth):
        wi = w_in[l]
        wr = wi[:, :c3].astype(BF16)
        wl = _pad_lora_cols(wi[:, c3:rwkv_in]).astype(BF16)
        wm = wi[:, rwkv_in:rwkv_in + c3].astype(BF16)
        wg = wi[:, rwkv_in + c3:].astype(BF16)
        zr, zl, zm, gates = _inproj(x2, row(norm1_g[l]), wr, wl, wm, wg, tm=256)

        consts = [row(rwkv_mu[l][:c3]), _pad_lora_cols(row(rwkv_mu[l][c3:])),
                  _pad_rows(w_decay_up[l], LANES), row(decay_bias[l]),
                  _pad_rows(w_aaa_up[l], LANES), row(aaa_bias[l]),
                  _pad_rows(w_gate_up[l], 2 * LANES),
                  row(rwkv_k_k[l]), row(rwkv_k_a[l]), row(rwkv_r_k[l]),
                  row(rwkv_ln_g[l]), row(rwkv_ln_b[l])]
        ya = _rwkv(zr, zl, consts, batch, seq).reshape(t, WIDTH)

        tile8 = lambda a: row(jnp.tile(a, N_HEADS))
        qa, ka, va = _moba_prep(zm, pos3, invf, tile8(q_norm_g[l]), tile8(k_norm_g[l]), mavg, batch, seq)
        yb = _moba_attn(qa, ka, va, batch, seq).reshape(t, WIDTH)

        x1, h2 = _merge(x2, ya, yb, gates, w_branch_a[l].astype(BF16), w_branch_b[l].astype(BF16),
                        w_out[l].astype(BF16), row(norm2_g[l]), tm=512)

        wu = w_ffn_up[l]
        x2 = _ffn(h2, x1, wu[:, :D_FF].astype(BF16), wu[:, D_FF:].astype(BF16), ffn_conv_w[l],
                  row(ffn_conv_b[l]), w_ffn_down[l].astype(BF16), tm=512, seq=seq)
    return x2.reshape(batch, seq, D_MODEL)
```

```python
import functools

import numpy as np
import jax
import jax.numpy as jnp
from jax import lax
from jax.experimental import pallas as pl
from jax.experimental.pallas import tpu as pltpu

F32 = jnp.float32
BF16 = jnp.bfloat16
HI = lax.Precision.HIGHEST

D_MODEL = 1024
HEAD_DIM = 64
N_HEADS = 8
WIDTH = N_HEADS * HEAD_DIM
DECAY_LORA = 64
AAA_LORA = 64
GATE_LORA = 160
LORA_PAD = 512
MOBA_BLOCK = 256
MOBA_TOPK = 3
ROT_DIM = HEAD_DIM // 4
ROPE_THETA = 500000.0
D_FF = 2816
NORM_EPS = 1e-6
GN_EPS = 64e-5
LANES = 128
CHUNK = 64
NEG_BIG = -32768.0
VMEM_LIMIT = 56 * 1024 * 1024


def _sigmoid(x):
    return 1.0 / (1.0 + jnp.exp(-x))


def _dot(a, b, precision=None):
    return jnp.dot(a, b, preferred_element_type=F32, precision=precision)


def _dot_nt(a, b, precision=None):
    return lax.dot_general(a, b, (((1,), (1,)), ((), ())), preferred_element_type=F32,
                           precision=precision)


def _dot_tn(a, b, precision=None):
    return lax.dot_general(a, b, (((0,), (0,)), ((), ())), preferred_element_type=F32,
                           precision=precision)


def _params(*sem):
    return pltpu.CompilerParams(dimension_semantics=sem, vmem_limit_bytes=VMEM_LIMIT)


def _const_spec(shape):
    nd = len(shape)
    return pl.BlockSpec(shape, lambda *_: (0,) * nd, pipeline_mode=pl.Buffered(1))


def _inproj_kernel(x_ref, g_ref, wr_ref, wl_ref, wm_ref, wg_ref,
                   zr_ref, zl_ref, zm_ref, gate_ref):
    x = x_ref[...]
    ms = jnp.mean(x * x, axis=-1, keepdims=True)
    h = (x * lax.rsqrt(ms + NORM_EPS) * g_ref[...]).astype(BF16)
    zr_ref[...] = _dot(h, wr_ref[...])
    zl_ref[...] = _dot(h, wl_ref[...])
    zm_ref[...] = _dot(h, wm_ref[...])
    gate_ref[...] = _sigmoid(_dot(h, wg_ref[...])).astype(BF16)


def _inproj(x2, g, wr, wl, wm, wg, tm):
    t = x2.shape[0]
    row = lambda n: pl.BlockSpec((tm, n), lambda i: (i, 0))
    return pl.pallas_call(
        _inproj_kernel,
        grid=(t // tm,),
        in_specs=[row(D_MODEL), _const_spec(g.shape), _const_spec(wr.shape), _const_spec(wl.shape),
                  _const_spec(wm.shape), _const_spec(wg.shape)],
        out_specs=[row(3 * WIDTH), row(LORA_PAD), row(3 * WIDTH), row(2 * D_MODEL)],
        out_shape=[jax.ShapeDtypeStruct((t, 3 * WIDTH), F32),
                   jax.ShapeDtypeStruct((t, LORA_PAD), F32),
                   jax.ShapeDtypeStruct((t, 3 * WIDTH), F32),
                   jax.ShapeDtypeStruct((t, 2 * D_MODEL), BF16)],
        compiler_params=_params("parallel"),
        name="inproj",
    )(x2, g, wr, wl, wm, wg)


RW_ROWS = 256
RW_GROUP = 256


def _rwkv_kernel(zr_ref, zrp_ref, zl_ref, zlp_ref, mur_ref, mul_ref, wd_ref, db_ref, wa_ref,
                 ab_ref, wg_ref, kk_ref, ka_ref, rk_ref, lng_ref, lnb_ref, cum_ref, hsum_ref, bd_ref,
                 o_ref, state_ref):
    c = pl.program_id(1)
    TL = zr_ref.shape[0]
    L = CHUNK
    G = RW_GROUP

    @pl.when(c == 0)
    def _():
        state_ref[...] = jnp.zeros_like(state_ref)

    def token_shift(z_ref, zp_ref, mu_ref):
        z = z_ref[...]
        prev = jnp.where(c == 0, 0.0, zp_ref[7:8, :])
        row = lax.broadcasted_iota(jnp.int32, z.shape, 0)
        zs = jnp.where(row == 0, prev, pltpu.roll(z, 1, axis=0))
        return z + mu_ref[...] * (zs - z)

    zr = token_shift(zr_ref, zrp_ref, mur_ref)
    zl = token_shift(zl_ref, zlp_ref, mul_ref)
    r = zr[:, 0:WIDTH]
    k = zr[:, WIDTH:2 * WIDTH]
    v = zr[:, 2 * WIDTH:3 * WIDTH]
    xw = zl[:, 0:LANES]
    xa = zl[:, LANES:2 * LANES]
    xg = zl[:, 2 * LANES:4 * LANES]

    dd = db_ref[...] + _dot(jnp.tanh(xw), wd_ref[...], HI)
    w_log = -(jnp.maximum(-dd, 0.0) + jnp.log(1.0 + jnp.exp(-jnp.abs(dd)))) - 0.5
    lw = -jnp.exp(w_log)
    asig = _sigmoid(ab_ref[...] + _dot(xa, wa_ref[...], HI))
    g = _dot(_sigmoid(xg), wg_ref[...], HI)
    hsum = hsum_ref[...]
    kkf = k * kk_ref[...]
    kk = kkf * lax.rsqrt(jnp.maximum(_dot((kkf * kkf).astype(BF16), hsum), 1e-24))
    kmod = k * (1.0 + (asig - 1.0) * ka_ref[...])
    bonus = _dot((r * kmod * rk_ref[...]).astype(BF16), hsum) * v
    b = kk * asig

    lw_hi = lw.astype(BF16)
    lw_lo = (lw - lw_hi.astype(F32)).astype(BF16)
    cums = _dot(cum_ref[...], lw_hi) + _dot(cum_ref[...], lw_lo)
    cw = cums[0:TL]
    cw_end = cums[TL:2 * TL]
    e_neg = jnp.exp(-cw)
    e_end = jnp.exp(cw_end - cw)
    a_til = (-kk * jnp.exp(cw - lw)).astype(BF16)
    r_til = (r * jnp.exp(cw)).astype(BF16)
    b_til = (b * e_neg).astype(BF16)
    k_til = (kmod * e_neg).astype(BF16)
    b_hat = (b * e_end).astype(BF16)
    k_hat = (kmod * e_end).astype(BF16)
    w_end = jnp.exp(cw_end)
    vb = v.astype(BF16)

    bdm = bd_ref[...]

    def bd(x):
        return jnp.concatenate([x.astype(BF16)] * (G // L), axis=0) * bdm

    row2 = lax.broadcasted_iota(jnp.int32, (2 * L, G), 0)
    lane_t = lax.broadcasted_iota(jnp.int32, (2 * L, G), 1) % L
    tri2 = lane_t < jnp.where(row2 < L, row2, row2 - L + 1)
    eye = (lax.broadcasted_iota(jnp.int32, (L, G), 1) % L
           == lax.broadcasted_iota(jnp.int32, (L, G), 0)).astype(F32)
    bdmask = bdm > 0

    y_rows = []
    for ci in range(TL // L):
        rs = slice(ci * L, (ci + 1) * L)
        y_groups = []
        for gi in range(WIDTH // G):
            ls = slice(gi * G, (gi + 1) * G)
            p = jnp.concatenate([a_til[rs, ls], r_til[rs, ls]], axis=0)
            ab_rb = jnp.where(tri2, _dot_nt(p, bd(b_til[rs, ls])), 0.0)
            ak_rk = jnp.where(tri2, _dot_nt(p, bd(k_til[rs, ls])), 0.0)
            a_ab = ab_rb[0:L]
            tinv = eye + a_ab
            pw = _dot(a_ab.astype(BF16), bd(a_ab))
            for _ in range(int(np.log2(L)) - 2):
                both = _dot(jnp.concatenate([pw, tinv], axis=0).astype(BF16), bd(pw))
                pw = both[0:L]
                tinv = tinv + both[L:2 * L]
            tinv = tinv + _dot(tinv.astype(BF16), bd(pw))

            kv = _dot(ak_rk.astype(BF16), bd(vb[rs, ls]))
            st = state_ref[gi]
            ps = _dot(p, st.astype(BF16))
            u = _dot(tinv.astype(BF16), bd(ps[0:L] + kv[0:L]))
            y_groups.append(ps[L:2 * L] + kv[L:2 * L] + _dot(ab_rb[L:2 * L].astype(BF16), bd(u)))
            upd = _dot_tn(jnp.concatenate([b_hat[rs, ls], k_hat[rs, ls]], axis=0),
                          jnp.concatenate([u.astype(BF16), vb[rs, ls]], axis=0))
            w_col = jnp.concatenate([w_end[rs, ls]] * (G // L), axis=0).T
            state_ref[gi] = w_col * st + jnp.where(bdmask, upd, 0.0)
        y_rows.append(jnp.concatenate(y_groups, axis=1))
    y = jnp.concatenate(y_rows, axis=0)

    inv_n = 1.0 / HEAD_DIM
    mu = _dot(y.astype(BF16), hsum) * inv_n
    yc = y - mu
    var = _dot((yc * yc).astype(BF16), hsum) * inv_n
    yn = yc * lax.rsqrt(var + GN_EPS) * lng_ref[...] + lnb_ref[...]
    o_ref[...] = ((yn + bonus) * g).astype(o_ref.dtype)


def _rwkv_masks():
    t = np.arange(RW_ROWS)
    same_chunk = (t[:, None] // CHUNK) == (t[None, :] // CHUNK)
    tri = same_chunk & (t[:, None] >= t[None, :])
    cum = np.concatenate([tri, same_chunk], axis=0)
    c = np.arange(WIDTH)
    head_sum = (c[:, None] // HEAD_DIM) == (c[None, :] // HEAD_DIM)
    g = np.arange(RW_GROUP)
    bd = (g[:, None] // HEAD_DIM) == (g[None, :] // HEAD_DIM)
    return [jnp.asarray(m, BF16) for m in (cum, head_sum, bd)]


def _rwkv(zr, zl, consts, batch, seq):
    TL = RW_ROWS
    nt = seq // TL
    cur = lambda n: pl.BlockSpec((None, TL, n), lambda b, c: (b, c, 0))
    prv = lambda n: pl.BlockSpec((None, 8, n), lambda b, c: (b, jnp.maximum(c * (TL // 8) - 1, 0), 0))
    zr3 = zr.reshape(batch, seq, 3 * WIDTH)
    zl3 = zl.reshape(batch, seq, LORA_PAD)
    consts = list(consts) + _rwkv_masks()
    return pl.pallas_call(
        _rwkv_kernel,
        grid=(batch, nt),
        in_specs=[cur(3 * WIDTH), prv(3 * WIDTH), cur(LORA_PAD), prv(LORA_PAD)]
                 + [_const_spec(a.shape) for a in consts],
        out_specs=pl.BlockSpec((None, TL, WIDTH), lambda b, c: (b, c, 0)),
        out_shape=jax.ShapeDtypeStruct((batch, seq, WIDTH), BF16),
        scratch_shapes=[pltpu.VMEM((WIDTH // RW_GROUP, RW_GROUP, RW_GROUP), F32)],
        compiler_params=_params("parallel", "arbitrary"),
        name="rwkv",
    )(zr3, zr3, zl3, zl3, *consts)


def _moba_prep_kernel(zm_ref, pos_ref, invf_ref, qg_ref, kg_ref, mavg_ref,
                      q_out, k_out, v_out, kmean_ref, *, n_blocks):
    i = pl.program_id(1)
    TB = zm_ref.shape[0]

    @pl.when(i == 0)
    def _():
        kmean_ref[...] = jnp.zeros_like(kmean_ref)

    zm = zm_ref[...]
    ang = pos_ref[...].astype(F32) * invf_ref[...]
    lane = lax.broadcasted_iota(jnp.int32, (TB, LANES), 1)
    lane_h = lane % HEAD_DIM
    cos = jnp.cos(ang)
    sin = jnp.sin(ang)
    half = ROT_DIM // 2
    sin_lo = jnp.where(lane_h < half, -sin, 0.0)
    sin_hi = jnp.where(lane_h >= half, sin, 0.0)
    rep = WIDTH // LANES
    cos = jnp.concatenate([cos] * rep, axis=-1)
    sin_lo = jnp.concatenate([sin_lo] * rep, axis=-1)
    sin_hi = jnp.concatenate([sin_hi] * rep, axis=-1)

    def norm_rope(t, gain):
        ms = _dot(t * t, mavg_ref[...], HI)
        t = t * lax.rsqrt(ms + NORM_EPS) * gain
        return (t * cos + pltpu.roll(t, WIDTH - half, axis=1) * sin_lo
                + pltpu.roll(t, half, axis=1) * sin_hi)

    q = norm_rope(zm[:, 0:WIDTH], qg_ref[...])
    k = norm_rope(zm[:, WIDTH:2 * WIDTH], kg_ref[...])
    v = zm[:, 2 * WIDTH:3 * WIDTH]

    kmean = kmean_ref[...]
    n_of_lane = lane - HEAD_DIM
    aux = (lane >= HEAD_DIM) & (n_of_lane < n_blocks)
    is_data = lane < HEAD_DIM
    km_data = lax.broadcasted_iota(jnp.int32, (LANES, LANES), 1) < HEAD_DIM

    for h in range(N_HEADS):
        ps = slice((h // 2) * LANES, (h // 2 + 1) * LANES)

        def head_base(t):
            p = t[:, ps]
            return p if h % 2 == 0 else pltpu.roll(p, HEAD_DIM, axis=1)

        qb, kb, vb = head_base(q), head_base(k), head_base(v)
        km = head_base(kmean)
        gate = _dot_nt(jnp.where(is_data, qb, 0.0), jnp.where(km_data, km, 0.0), HI)
        gsel = jnp.where(aux & (n_of_lane < i), gate, -jnp.inf)
        picked = lane < 0
        for _ in range(MOBA_TOPK):
            m = jnp.max(gsel, axis=-1, keepdims=True)
            idx = jnp.min(jnp.where(gsel == m, lane, 2 * LANES), axis=-1, keepdims=True)
            pick = lane == idx
            picked = picked | (pick & (m > -jnp.inf))
            gsel = jnp.where(pick, -jnp.inf, gsel)
        keep = picked | (n_of_lane == i)
        bias = jnp.where(aux & jnp.logical_not(keep), NEG_BIG, 0.0)
        q_out[h] = jnp.where(is_data, qb * (HEAD_DIM ** -0.5), bias).astype(BF16)
        k_out[h] = jnp.where(is_data, kb, jnp.where(n_of_lane == i, 1.0, 0.0)).astype(BF16)
        v_out[h] = jnp.where(is_data, vb, jnp.where(lane == HEAD_DIM, 1.0, 0.0)).astype(BF16)

    rowk = lax.broadcasted_iota(jnp.int32, kmean.shape, 0)
    kmean_ref[...] = jnp.where(rowk == HEAD_DIM + i, jnp.mean(k, axis=0, keepdims=True), kmean)


def _moba_prep(zm, pos3, invf, qg, kg, mavg, batch, seq):
    TB = MOBA_BLOCK
    nb = seq // TB
    zm3 = zm.reshape(batch, seq, 3 * WIDTH)
    aug = pl.BlockSpec((None, N_HEADS, TB, LANES), lambda b, i: (b, 0, i, 0))
    aug_shape = jax.ShapeDtypeStruct((batch, N_HEADS, seq, LANES), BF16)
    return pl.pallas_call(
        functools.partial(_moba_prep_kernel, n_blocks=nb),
        grid=(batch, nb),
        in_specs=[pl.BlockSpec((None, TB, 3 * WIDTH), lambda b, i: (b, i, 0)),
                  pl.BlockSpec((None, TB, 1), lambda b, i: (b, i, 0)),
                  _const_spec(invf.shape), _const_spec(qg.shape), _const_spec(kg.shape),
                  _const_spec(mavg.shape)],
        out_specs=[aug, aug, aug],
        out_shape=[aug_shape, aug_shape, aug_shape],
        scratch_shapes=[pltpu.VMEM((LANES, WIDTH), F32)],
        compiler_params=_params("parallel", "arbitrary"),
        name="moba_prep",
    )(zm3, pos3, invf, qg, kg, mavg)


def _moba_attn_kernel(q_ref, k_ref, v_ref, o_ref, acc_ref, m_ref, sa_ref, sb_ref):
    i = pl.program_id(2)
    TB = q_ref.shape[1]
    HP = q_ref.shape[0]
    row = lax.broadcasted_iota(jnp.int32, (TB, TB), 0)
    col = lax.broadcasted_iota(jnp.int32, (TB, TB), 1)
    causal = col <= row
    own = pl.multiple_of(i * TB, TB)

    def wide(m):
        return jnp.concatenate([m] * (TB // LANES), axis=1)

    for hh in range(HP):
        sa_ref[hh] = jnp.where(causal, _dot_nt(q_ref[hh], k_ref[hh, pl.ds(own, TB), :]), -1e30)
        m_ref[hh] = jnp.full((TB, LANES), -1e30, F32)
        acc_ref[hh] = jnp.zeros((TB, LANES), F32)

    def step(t, cur_ref, nxt_ref):
        voff = pl.multiple_of(jnp.where(t == 0, i, t - 1) * TB, TB)
        if nxt_ref is not None:
            koff = pl.multiple_of(jnp.minimum(t, jnp.maximum(i - 1, 0)) * TB, TB)
            for hh in range(HP):
                nxt_ref[hh] = _dot_nt(q_ref[hh], k_ref[hh, pl.ds(koff, TB), :])
        for hh in range(HP):
            s = cur_ref[hh]
            m_old = m_ref[hh]
            m_new = jnp.maximum(m_old, jnp.max(s, axis=-1, keepdims=True))
            m_ref[hh] = m_new
            p = jnp.exp(s - wide(m_new)).astype(BF16)
            acc_ref[hh] = acc_ref[hh] * jnp.exp(m_old - m_new) + _dot(p, v_ref[hh, pl.ds(voff, TB), :])

    def step_pair(jj, carry):
        step(2 * jj, sa_ref, sb_ref)
        step(2 * jj + 1, sb_ref, sa_ref)
        return carry

    lax.fori_loop(0, (i + 1) // 2, step_pair, 0)

    @pl.when(i % 2 == 0)
    def _():
        step(i, sa_ref, None)

    outs = []
    for hh in range(HP):
        acc = acc_ref[hh]
        outs.append(acc / acc[:, HEAD_DIM:HEAD_DIM + 1])
    lane = lax.broadcasted_iota(jnp.int32, (TB, LANES), 1)
    o_ref[...] = jnp.where(lane < HEAD_DIM, outs[0],
                           pltpu.roll(outs[1], HEAD_DIM, axis=1)).astype(o_ref.dtype)


def _moba_attn(qa, ka, va, batch, seq):
    TB = MOBA_BLOCK
    nb = seq // TB
    HP = 2
    return pl.pallas_call(
        _moba_attn_kernel,
        grid=(batch, N_HEADS // HP, nb),
        in_specs=[pl.BlockSpec((None, HP, TB, LANES), lambda b, p, i: (b, p, i, 0)),
                  pl.BlockSpec((None, HP, seq, LANES), lambda b, p, i: (b, p, 0, 0)),
                  pl.BlockSpec((None, HP, seq, LANES), lambda b, p, i: (b, p, 0, 0))],
        out_specs=pl.BlockSpec((None, TB, LANES), lambda b, p, i: (b, i, p)),
        out_shape=jax.ShapeDtypeStruct((batch, seq, WIDTH), BF16),
        scratch_shapes=[pltpu.VMEM((HP, TB, LANES), F32), pltpu.VMEM((HP, TB, LANES), F32),
                        pltpu.VMEM((HP, TB, TB), F32), pltpu.VMEM((HP, TB, TB), F32)],
        compiler_params=_params("parallel", "parallel", "arbitrary"),
        name="moba_attn",
    )(qa, ka, va)


def _merge_kernel(x_ref, ya_ref, yb_ref, gate_ref, wa_ref, wb_ref, wo_ref, g2_ref, x1_ref, h2_ref):
    ua = _dot(ya_ref[...], wa_ref[...])
    ub = _dot(yb_ref[...], wb_ref[...])
    gate = gate_ref[...].astype(F32)
    mix = (gate[:, 0:D_MODEL] * ua + gate[:, D_MODEL:] * ub).astype(BF16)
    x1 = x_ref[...] + _dot(mix, wo_ref[...])
    x1_ref[...] = x1
    ms = jnp.mean(x1 * x1, axis=-1, keepdims=True)
    h2_ref[...] = (x1 * lax.rsqrt(ms + NORM_EPS) * g2_ref[...]).astype(BF16)


def _merge(x2, ya, yb, gates, wa, wb, wo, g2, tm):
    t = x2.shape[0]
    row = lambda n: pl.BlockSpec((tm, n), lambda i: (i, 0))
    return pl.pallas_call(
        _merge_kernel,
        grid=(t // tm,),
        in_specs=[row(D_MODEL), row(WIDTH), row(WIDTH), row(2 * D_MODEL), _const_spec(wa.shape),
                  _const_spec(wb.shape), _const_spec(wo.shape), _const_spec(g2.shape)],
        out_specs=[row(D_MODEL), row(D_MODEL)],
        out_shape=[jax.ShapeDtypeStruct((t, D_MODEL), F32), jax.ShapeDtypeStruct((t, D_MODEL), BF16)],
        compiler_params=_params("parallel"),
        name="merge",
    )(x2, ya, yb, gates, wa, wb, wo, g2)


FFN_HALO = 16
FFN_SPLIT = 2


def _ffn_kernel(h_ref, hp_ref, x1_ref, wua_ref, wub_ref, cw_ref, cb_ref, wd_ref, o_ref, *, tiles_per_seq):
    i = pl.program_id(0)
    tm = h_ref.shape[0]
    h = h_ref[...]
    halo = jnp.where(i % tiles_per_seq == 0, jnp.zeros_like(hp_ref[...]), hp_ref[...])
    h_ext = jnp.concatenate([halo, h], axis=0)
    fc = D_FF // FFN_SPLIT
    acc = x1_ref[...]
    for s in range(FFN_SPLIT):
        cs = slice(s * fc, (s + 1) * fc)
        a = _dot(h_ext, wua_ref[:, cs])
        b = _dot(h, wub_ref[:, cs])
        conv = (a[FFN_HALO:, :] * cw_ref[2:3, cs] + a[FFN_HALO - 1:FFN_HALO - 1 + tm, :] * cw_ref[1:2, cs]
                + a[FFN_HALO - 2:FFN_HALO - 2 + tm, :] * cw_ref[0:1, cs] + cb_ref[:, cs])
        gelu = 0.5 * conv * (1.0 + lax.erf(conv * (2.0 ** -0.5)))
        acc = acc + _dot((gelu * b).astype(BF16), wd_ref[cs, :])
    o_ref[...] = acc


def _ffn(h2, x1, wua, wub, cw, cb, wd, tm, seq):
    t = h2.shape[0]
    row = lambda n: pl.BlockSpec((tm, n), lambda i: (i, 0))
    halo = pl.BlockSpec((FFN_HALO, D_MODEL), lambda i: (jnp.maximum(i * (tm // FFN_HALO) - 1, 0), 0))
    return pl.pallas_call(
        functools.partial(_ffn_kernel, tiles_per_seq=seq // tm),
        grid=(t // tm,),
        in_specs=[row(D_MODEL), halo, row(D_MODEL), _const_spec(wua.shape), _const_spec(wub.shape),
                  _const_spec(cw.shape), _const_spec(cb.shape), _const_spec(wd.shape)],
        out_specs=row(D_MODEL),
        out_shape=jax.ShapeDtypeStruct((t, D_MODEL), F32),
        compiler_params=_params("parallel"),
        name="ffn",
    )(h2, h2, x1, wua, wub, cw, cb, wd)


def _pad_lora_cols(a):
    z = lambda n: jnp.zeros(a.shape[:-1] + (n,), a.dtype)
    o1, o2 = DECAY_LORA, DECAY_LORA + AAA_LORA
    return jnp.concatenate([a[..., :o1], z(LANES - DECAY_LORA), a[..., o1:o2], z(LANES - AAA_LORA),
                            a[..., o2:], z(2 * LANES - GATE_LORA)], axis=-1)


def _pad_rows(a, n):
    return jnp.concatenate([a, jnp.zeros((n - a.shape[0],) + a.shape[1:], a.dtype)], axis=0)


def _rope_inv_freq():
    half = ROT_DIM // 2
    lane = np.arange(LANES) % HEAD_DIM
    f = ROPE_THETA ** (-(lane % half).astype(np.float64) / half)
    return jnp.asarray(np.where(lane < ROT_DIM, f, 0.0)[None, :], F32)


def _head_mean_matrix():
    return jnp.asarray(np.kron(np.eye(N_HEADS), np.full((HEAD_DIM, HEAD_DIM), 1.0 / HEAD_DIM)), F32)


def kernel(x, positions, norm1_g, w_in, rwkv_mu, w_decay_up, decay_bias, w_aaa_up, aaa_bias, w_gate_up, rwkv_k_k, rwkv_k_a, rwkv_r_k, rwkv_ln_g, rwkv_ln_b, q_norm_g, k_norm_g, w_branch_a, w_branch_b, w_out, norm2_g, w_ffn_up, ffn_conv_w, ffn_conv_b, w_ffn_down):
    batch, seq, _ = x.shape
    depth = norm1_g.shape[0]
    assert seq % MOBA_BLOCK == 0 and seq % 512 == 0
    t = batch * seq
    row = lambda a: a.reshape(1, -1)
    c3 = 3 * WIDTH
    rwkv_in = c3 + DECAY_LORA + AAA_LORA + GATE_LORA
    pos3 = positions.reshape(batch, seq, 1)
    invf = _rope_inv_freq()
    mavg = _head_mean_matrix()

    x2 = x.reshape(t, D_MODEL)
    for l in range(depth):
        wi = w_in[l]
        wr = wi[:, :c3].astype(BF16)
        wl = _pad_lora_cols(wi[:, c3:rwkv_in]).astype(BF16)
        wm = wi[:, rwkv_in:rwkv_in + c3].astype(BF16)
        wg = wi[:, rwkv_in + c3:].astype(BF16)
        zr, zl, zm, gates = _inproj(x2, row(norm1_g[l]), wr, wl, wm, wg, tm=256)

        consts = [row(rwkv_mu[l][:c3]), _pad_lora_cols(row(rwkv_mu[l][c3:])),
                  _pad_rows(w_decay_up[l], LANES), row(decay_bias[l]),
                  _pad_rows(w_aaa_up[l], LANES), row(aaa_bias[l]),
                  _pad_rows(w_gate_up[l], 2 * LANES),
                  row(rwkv_k_k[l]), row(rwkv_k_a[l]), row(rwkv_r_k[l]),
                  row(rwkv_ln_g[l]), row(rwkv_ln_b[l])]
        ya = _rwkv(zr, zl, consts, batch, seq).reshape(t, WIDTH)

        tile8 = lambda a: row(jnp.tile(a, N_HEADS))
        qa, ka, va = _moba_prep(zm, pos3, invf, tile8(q_norm_g[l]), tile8(k_norm_g[l]), mavg, batch, seq)
        yb = _moba_attn(qa, ka, va, batch, seq).reshape(t, WIDTH)

        x1, h2 = _merge(x2, ya, yb, gates, w_branch_a[l].astype(BF16), w_branch_b[l].astype(BF16),
                        w_out[l].astype(BF16), row(norm2_g[l]), tm=512)

        wu = w_ffn_up[l]
        x2 = _ffn(h2, x1, wu[:, :D_FF].astype(BF16), wu[:, D_FF:].astype(BF16), ffn_conv_w[l],
                  row(ffn_conv_b[l]), w_ffn_down[l].astype(BF16), tm=512, seq=seq)
    return x2.reshape(batch, seq, D_MODEL)
```

```python
import functools

import numpy as np
import jax
import jax.numpy as jnp
from jax import lax
from jax.experimental import pallas as pl
from jax.experimental.pallas import tpu as pltpu

F32 = jnp.float32
BF16 = jnp.bfloat16
HI = lax.Precision.HIGHEST

D_MODEL = 1024
HEAD_DIM = 64
N_HEADS = 8
WIDTH = N_HEADS * HEAD_DIM
DECAY_LORA = 64
AAA_LORA = 64
GATE_LORA = 160
LORA_PAD = 512
MOBA_BLOCK = 256
MOBA_TOPK = 3
ROT_DIM = HEAD_DIM // 4
ROPE_THETA = 500000.0
D_FF = 2816
NORM_EPS = 1e-6
GN_EPS = 64e-5
LANES = 128
CHUNK = 64
NEG_BIG = -32768.0
VMEM_LIMIT = 56 * 1024 * 1024


def _sigmoid(x):
    return 1.0 / (1.0 + jnp.exp(-x))


def _dot(a, b, precision=None):
    return jnp.dot(a, b, preferred_element_type=F32, precision=precision)


def _dot_nt(a, b, precision=None):
    return lax.dot_general(a, b, (((1,), (1,)), ((), ())), preferred_element_type=F32,
                           precision=precision)


def _dot_tn(a, b, precision=None):
    return lax.dot_general(a, b, (((0,), (0,)), ((), ())), preferred_element_type=F32,
                           precision=precision)


def _params(*sem):
    return pltpu.CompilerParams(dimension_semantics=sem, vmem_limit_bytes=VMEM_LIMIT)


def _const_spec(shape):
    nd = len(shape)
    return pl.BlockSpec(shape, lambda *_: (0,) * nd, pipeline_mode=pl.Buffered(1))


def _inproj_kernel(x_ref, g_ref, wr_ref, wl_ref, wm_ref, wg_ref,
                   zr_ref, zl_ref, zm_ref, gate_ref):
    x = x_ref[...]
    ms = jnp.mean(x * x, axis=-1, keepdims=True)
    h = (x * lax.rsqrt(ms + NORM_EPS) * g_ref[...]).astype(BF16)
    zr_ref[...] = _dot(h, wr_ref[...])
    zl_ref[...] = _dot(h, wl_ref[...])
    zm_ref[...] = _dot(h, wm_ref[...])
    gate_ref[...] = _sigmoid(_dot(h, wg_ref[...])).astype(BF16)


def _inproj(x2, g, wr, wl, wm, wg, tm):
    t = x2.shape[0]
    row = lambda n: pl.BlockSpec((tm, n), lambda i: (i, 0))
    return pl.pallas_call(
        _inproj_kernel,
        grid=(t // tm,),
        in_specs=[row(D_MODEL), _const_spec(g.shape), _const_spec(wr.shape), _const_spec(wl.shape),
                  _const_spec(wm.shape), _const_spec(wg.shape)],
        out_specs=[row(3 * WIDTH), row(LORA_PAD), row(3 * WIDTH), row(2 * D_MODEL)],
        out_shape=[jax.ShapeDtypeStruct((t, 3 * WIDTH), F32),
                   jax.ShapeDtypeStruct((t, LORA_PAD), F32),
                   jax.ShapeDtypeStruct((t, 3 * WIDTH), F32),
                   jax.ShapeDtypeStruct((t, 2 * D_MODEL), BF16)],
        compiler_params=_params("parallel"),
        name="inproj",
    )(x2, g, wr, wl, wm, wg)


RW_ROWS = 256
RW_GROUP = 256


def _rwkv_kernel(zr_ref, zrp_ref, zl_ref, zlp_ref, mur_ref, mul_ref, wd_ref, db_ref, wa_ref,
                 ab_ref, wg_ref, kk_ref, ka_ref, rk_ref, lng_ref, lnb_ref, cum_ref, hsum_ref, bd_ref,
                 o_ref, state_ref):
    c = pl.program_id(1)
    TL = zr_ref.shape[0]
    L = CHUNK
    G = RW_GROUP

    @pl.when(c == 0)
    def _():
        state_ref[...] = jnp.zeros_like(state_ref)

    def token_shift(z_ref, zp_ref, mu_ref):
        z = z_ref[...]
        prev = jnp.where(c == 0, 0.0, zp_ref[7:8, :])
        row = lax.broadcasted_iota(jnp.int32, z.shape, 0)
        zs = jnp.where(row == 0, prev, pltpu.roll(z, 1, axis=0))
        return z + mu_ref[...] * (zs - z)

    zr = token_shift(zr_ref, zrp_ref, mur_ref)
    zl = token_shift(zl_ref, zlp_ref, mul_ref)
    r = zr[:, 0:WIDTH]
    k = zr[:, WIDTH:2 * WIDTH]
    v = zr[:, 2 * WIDTH:3 * WIDTH]
    xw = zl[:, 0:LANES]
    xa = zl[:, LANES:2 * LANES]
    xg = zl[:, 2 * LANES:4 * LANES]

    dd = db_ref[...] + _dot(jnp.tanh(xw).astype(BF16), wd_ref[...])
    w_log = -(jnp.maximum(-dd, 0.0) + jnp.log(1.0 + jnp.exp(-jnp.abs(dd)))) - 0.5
    lw = -jnp.exp(w_log)
    asig = _sigmoid(ab_ref[...] + _dot(xa.astype(BF16), wa_ref[...]))
    g = _dot(_sigmoid(xg).astype(BF16), wg_ref[...])
    hsum = hsum_ref[...]
    kkf = k * kk_ref[...]
    kk = kkf * lax.rsqrt(jnp.maximum(_dot((kkf * kkf).astype(BF16), hsum), 1e-24))
    kmod = k * (1.0 + (asig - 1.0) * ka_ref[...])
    bonus = _dot((r * kmod * rk_ref[...]).astype(BF16), hsum) * v
    b = kk * asig

    lw_hi = lw.astype(BF16)
    lw_lo = (lw - lw_hi.astype(F32)).astype(BF16)
    cums = _dot(cum_ref[...], lw_hi) + _dot(cum_ref[...], lw_lo)
    cw = cums[0:TL]
    cw_end = cums[TL:2 * TL]
    e_neg = jnp.exp(-cw)
    e_end = jnp.exp(cw_end - cw)
    a_til = (-kk * jnp.exp(cw - lw)).astype(BF16)
    r_til = (r * jnp.exp(cw)).astype(BF16)
    b_til = (b * e_neg).astype(BF16)
    k_til = (kmod * e_neg).astype(BF16)
    b_hat = (b * e_end).astype(BF16)
    k_hat = (kmod * e_end).astype(BF16)
    w_end = jnp.exp(cw_end)
    vb = v.astype(BF16)

    bdm = bd_ref[...]

    def bd(x):
        return jnp.concatenate([x.astype(BF16)] * (G // L), axis=0) * bdm

    row2 = lax.broadcasted_iota(jnp.int32, (2 * L, G), 0)
    lane_t = lax.broadcasted_iota(jnp.int32, (2 * L, G), 1) % L
    tri2 = lane_t < jnp.where(row2 < L, row2, row2 - L + 1)
    eye = (lax.broadcasted_iota(jnp.int32, (L, G), 1) % L
           == lax.broadcasted_iota(jnp.int32, (L, G), 0)).astype(F32)
    bdmask = bdm > 0

    n_chunks, n_groups = TL // L, WIDTH // G
    pairs = [(ci, gi) for ci in range(n_chunks) for gi in range(n_groups)]

    def blk(x, cg):
        ci, gi = cg
        return x[ci * L:(ci + 1) * L, gi * G:(gi + 1) * G]

    p = {cg: jnp.concatenate([blk(a_til, cg), blk(r_til, cg)], axis=0) for cg in pairs}
    ab_rb = {cg: jnp.where(tri2, _dot_nt(p[cg], bd(blk(b_til, cg))), 0.0) for cg in pairs}
    ak_rk = {cg: jnp.where(tri2, _dot_nt(p[cg], bd(blk(k_til, cg))), 0.0) for cg in pairs}
    tinv = {cg: eye + ab_rb[cg][0:L] for cg in pairs}
    pw = {cg: _dot(ab_rb[cg][0:L].astype(BF16), bd(ab_rb[cg][0:L])) for cg in pairs}
    for _ in range(int(np.log2(L)) - 2):
        both = {cg: _dot(jnp.concatenate([pw[cg], tinv[cg]], axis=0).astype(BF16), bd(pw[cg])) for cg in pairs}
        pw = {cg: both[cg][0:L] for cg in pairs}
        tinv = {cg: tinv[cg] + both[cg][L:2 * L] for cg in pairs}
    tinv = {cg: (tinv[cg] + _dot(tinv[cg].astype(BF16), bd(pw[cg]))).astype(BF16) for cg in pairs}
    kv = {cg: _dot(ak_rk[cg].astype(BF16), bd(blk(vb, cg))) for cg in pairs}
    a_rb = {cg: ab_rb[cg][L:2 * L].astype(BF16) for cg in pairs}
    bk_t = {cg: jnp.concatenate([blk(b_hat, cg), blk(k_hat, cg)], axis=0).T.astype(BF16) for cg in pairs}
    w_col = {cg: jnp.concatenate([blk(w_end, cg)] * (G // L), axis=0).T for cg in pairs}

    st = [state_ref[gi] for gi in range(n_groups)]
    y_rows = []
    for ci in range(n_chunks):
        cgs = [(ci, gi) for gi in range(n_groups)]
        ps = [_dot(p[cg], st[cg[1]].astype(BF16)) for cg in cgs]
        u = [_dot(tinv[cg], bd(ps[gi][0:L] + kv[cg][0:L])) for gi, cg in enumerate(cgs)]
        upd = [_dot(bk_t[cg], jnp.concatenate([u[gi].astype(BF16), blk(vb, cg)], axis=0))
               for gi, cg in enumerate(cgs)]
        st = [w_col[cg] * st[gi] + jnp.where(bdmask, upd[gi], 0.0) for gi, cg in enumerate(cgs)]
        y_rows.append(jnp.concatenate(
            [ps[gi][L:2 * L] + kv[cg][L:2 * L] + _dot(a_rb[cg], bd(u[gi])) for gi, cg in enumerate(cgs)],
            axis=1))
    for gi in range(n_groups):
        state_ref[gi] = st[gi]
    y = jnp.concatenate(y_rows, axis=0)

    inv_n = 1.0 / HEAD_DIM
    mu = _dot(y.astype(BF16), hsum) * inv_n
    yc = y - mu
    var = _dot((yc * yc).astype(BF16), hsum) * inv_n
    yn = yc * lax.rsqrt(var + GN_EPS) * lng_ref[...] + lnb_ref[...]
    o_ref[...] = ((yn + bonus) * g).astype(o_ref.dtype)


def _rwkv_masks():
    t = np.arange(RW_ROWS)
    same_chunk = (t[:, None] // CHUNK) == (t[None, :] // CHUNK)
    tri = same_chunk & (t[:, None] >= t[None, :])
    cum = np.concatenate([tri, same_chunk], axis=0)
    c = np.arange(WIDTH)
    head_sum = (c[:, None] // HEAD_DIM) == (c[None, :] // HEAD_DIM)
    g = np.arange(RW_GROUP)
    bd = (g[:, None] // HEAD_DIM) == (g[None, :] // HEAD_DIM)
    return [jnp.asarray(m, BF16) for m in (cum, head_sum, bd)]


def _rwkv(zr, zl, consts, batch, seq):
    TL = RW_ROWS
    nt = seq // TL
    cur = lambda n: pl.BlockSpec((None, TL, n), lambda b, c: (b, c, 0))
    prv = lambda n: pl.BlockSpec((None, 8, n), lambda b, c: (b, jnp.maximum(c * (TL // 8) - 1, 0), 0))
    zr3 = zr.reshape(batch, seq, 3 * WIDTH)
    zl3 = zl.reshape(batch, seq, LORA_PAD)
    consts = list(consts) + _rwkv_masks()
    return pl.pallas_call(
        _rwkv_kernel,
        grid=(batch, nt),
        in_specs=[cur(3 * WIDTH), prv(3 * WIDTH), cur(LORA_PAD), prv(LORA_PAD)]
                 + [_const_spec(a.shape) for a in consts],
        out_specs=pl.BlockSpec((None, TL, WIDTH), lambda b, c: (b, c, 0)),
        out_shape=jax.ShapeDtypeStruct((batch, seq, WIDTH), BF16),
        scratch_shapes=[pltpu.VMEM((WIDTH // RW_GROUP, RW_GROUP, RW_GROUP), F32)],
        compiler_params=_params("parallel", "arbitrary"),
        name="rwkv",
    )(zr3, zr3, zl3, zl3, *consts)


AUX_BLOCKS = LANES // N_HEADS
HALF_HEADS = N_HEADS // 2


def _moba_prep_kernel(zm_ref, pos_ref, invf_ref, ropep_ref, qg_ref, kg_ref, hsum_ref,
                      q_out, k_out, v_out, km_ref, *, n_blocks):
    i = pl.program_id(1)
    TB = zm_ref.shape[0]

    @pl.when(i == 0)
    def _():
        km_ref[...] = jnp.zeros_like(km_ref)

    zm = zm_ref[...]
    ang = invf_ref[...] * pos_ref[...].astype(F32)
    cs = jnp.concatenate([jnp.cos(ang), jnp.sin(ang)], axis=0)
    cs_hi = cs.astype(BF16).astype(F32)
    tab = _dot_tn(jnp.concatenate([cs_hi, cs - cs_hi], axis=0), ropep_ref[...])
    lane = lax.broadcasted_iota(jnp.int32, (TB, LANES), 1)
    rep = WIDTH // LANES
    half = ROT_DIM // 2
    cos = jnp.concatenate([tab[:, 0:LANES] + jnp.where(lane % HEAD_DIM >= ROT_DIM, 1.0, 0.0)] * rep, axis=-1)
    sin_lo = jnp.concatenate([tab[:, LANES:2 * LANES]] * rep, axis=-1)
    sin_hi = jnp.concatenate([tab[:, 2 * LANES:3 * LANES]] * rep, axis=-1)
    hsum = hsum_ref[...]

    def norm_rope(t, gain):
        ms = _dot((t * t).astype(BF16), hsum) * (1.0 / HEAD_DIM)
        t = t * lax.rsqrt(ms + NORM_EPS) * gain
        return (t * cos + pltpu.roll(t, WIDTH - half, axis=1) * sin_lo
                + pltpu.roll(t, half, axis=1) * sin_hi)

    q = norm_rope(zm[:, 0:WIDTH], qg_ref[...])
    k = norm_rope(zm[:, WIDTH:2 * WIDTH], kg_ref[...])
    v = zm[:, 2 * WIDTH:3 * WIDTH]

    km = km_ref[...]
    gate = _dot_nt(km, q, HI).reshape(N_HEADS, AUX_BLOCKS, TB)
    n_idx = lax.broadcasted_iota(jnp.int32, gate.shape, 1)
    gsel = jnp.where(n_idx < i, gate, -jnp.inf)
    picked = n_idx < 0
    for _ in range(MOBA_TOPK):
        m = jnp.max(gsel, axis=1, keepdims=True)
        idx = jnp.min(jnp.where(gsel == m, n_idx, AUX_BLOCKS), axis=1, keepdims=True)
        pick = n_idx == idx
        picked = picked | (pick & (m > -jnp.inf))
        gsel = jnp.where(pick, -jnp.inf, gsel)
    keep = picked | (n_idx == i) | (n_idx >= n_blocks)
    bias = jnp.where(keep, 0.0, NEG_BIG).reshape(LANES, TB).T
    bias_by_half = [pltpu.roll(bias, HEAD_DIM, axis=1), bias]

    rowk = lax.broadcasted_iota(jnp.int32, km.shape, 0)
    lanek = lax.broadcasted_iota(jnp.int32, km.shape, 1)
    mine = (rowk % AUX_BLOCKS == i) & (lanek // HEAD_DIM == rowk // AUX_BLOCKS)
    km_ref[...] = jnp.where(mine, jnp.mean(k, axis=0, keepdims=True), km)

    is_data = lane < HEAD_DIM

    for h in range(N_HEADS):
        ps = slice((h // 2) * LANES, (h // 2 + 1) * LANES)

        def head_base(t):
            p = t[:, ps]
            return p if h % 2 == 0 else pltpu.roll(p, HEAD_DIM, axis=1)

        qb, kb, vb = head_base(q), head_base(k), head_base(v)
        own_lane = HEAD_DIM + (h % HALF_HEADS) * AUX_BLOCKS + i
        q_out[h] = jnp.where(is_data, qb * (HEAD_DIM ** -0.5), bias_by_half[h // HALF_HEADS]).astype(BF16)
        k_out[h] = jnp.where(is_data, kb, jnp.where(lane == own_lane, 1.0, 0.0)).astype(BF16)
        v_out[h] = jnp.where(is_data, vb, jnp.where(lane == HEAD_DIM, 1.0, 0.0)).astype(BF16)


def _rope_tables():
    half = ROT_DIM // 2
    invf = ROPE_THETA ** (-np.arange(half, dtype=np.float64) / half)
    lane = np.arange(LANES) % HEAD_DIM
    f = np.arange(half)[:, None]
    p_cos = ((lane[None, :] < ROT_DIM) & (lane[None, :] % half == f)).astype(np.float64)
    p_lo = -(lane[None, :] == f).astype(np.float64)
    p_hi = (lane[None, :] == f + half).astype(np.float64)
    z = np.zeros_like(p_cos)
    cos_rows = np.concatenate([p_cos, z, z], axis=1)
    sin_rows = np.concatenate([z, p_lo, p_hi], axis=1)
    expand = np.concatenate([cos_rows, sin_rows, cos_rows, sin_rows], axis=0)
    return (jnp.asarray(np.broadcast_to(invf[:, None], (half, MOBA_BLOCK)), F32), jnp.asarray(expand, F32))


def _moba_prep(zm, positions, qg, kg, hsum, batch, seq):
    TB = MOBA_BLOCK
    nb = seq // TB
    assert nb <= AUX_BLOCKS
    zm3 = zm.reshape(batch, seq, 3 * WIDTH)
    pos4 = positions.reshape(batch, nb, 1, TB)
    invf, expand = _rope_tables()
    aug = pl.BlockSpec((None, N_HEADS, TB, LANES), lambda b, i: (b, 0, i, 0))
    aug_shape = jax.ShapeDtypeStruct((batch, N_HEADS, seq, LANES), BF16)
    return pl.pallas_call(
        functools.partial(_moba_prep_kernel, n_blocks=nb),
        grid=(batch, nb),
        in_specs=[pl.BlockSpec((None, TB, 3 * WIDTH), lambda b, i: (b, i, 0)),
                  pl.BlockSpec((None, None, 1, TB), lambda b, i: (b, i, 0, 0)),
                  _const_spec(invf.shape), _const_spec(expand.shape), _const_spec(qg.shape),
                  _const_spec(kg.shape), _const_spec(hsum.shape)],
        out_specs=[aug, aug, aug],
        out_shape=[aug_shape, aug_shape, aug_shape],
        scratch_shapes=[pltpu.VMEM((LANES, WIDTH), F32)],
        compiler_params=_params("parallel", "arbitrary"),
        name="moba_prep",
    )(zm3, pos4, invf, expand, qg, kg, hsum)


def _moba_attn_kernel(q_ref, k_ref, v_ref, o_ref, acc_ref, m_ref, sa_ref, sb_ref):
    i = pl.program_id(2)
    TB = q_ref.shape[1]
    HP = q_ref.shape[0]
    row = lax.broadcasted_iota(jnp.int32, (TB, TB), 0)
    col = lax.broadcasted_iota(jnp.int32, (TB, TB), 1)
    causal = col <= row
    own = pl.multiple_of(i * TB, TB)

    def wide(m):
        return jnp.concatenate([m] * (TB // LANES), axis=1)

    for hh in range(HP):
        sa_ref[hh] = jnp.where(causal, _dot_nt(q_ref[hh], k_ref[hh, pl.ds(own, TB), :]), -1e30)
        m_ref[hh] = jnp.full((TB, LANES), -1e30, F32)
        acc_ref[hh] = jnp.zeros((TB, LANES), F32)

    def step(t, cur_ref, nxt_ref):
        voff = pl.multiple_of(jnp.where(t == 0, i, t - 1) * TB, TB)
        if nxt_ref is not None:
            koff = pl.multiple_of(jnp.minimum(t, jnp.maximum(i - 1, 0)) * TB, TB)
            for hh in range(HP):
                nxt_ref[hh] = _dot_nt(q_ref[hh], k_ref[hh, pl.ds(koff, TB), :])
        for hh in range(HP):
            s = cur_ref[hh]
            m_old = m_ref[hh]
            m_new = jnp.maximum(m_old, jnp.max(s, axis=-1, keepdims=True))
            m_ref[hh] = m_new
            p = jnp.exp(s - wide(m_new)).astype(BF16)
            acc_ref[hh] = acc_ref[hh] * jnp.exp(m_old - m_new) + _dot(p, v_ref[hh, pl.ds(voff, TB), :])

    def step_pair(jj, carry):
        step(2 * jj, sa_ref, sb_ref)
        step(2 * jj + 1, sb_ref, sa_ref)
        return carry

    lax.fori_loop(0, (i + 1) // 2, step_pair, 0)

    @pl.when(i % 2 == 0)
    def _():
        step(i, sa_ref, None)

    outs = []
    for hh in range(HP):
        acc = acc_ref[hh]
        outs.append(acc / acc[:, HEAD_DIM:HEAD_DIM + 1])
    lane = lax.broadcasted_iota(jnp.int32, (TB, LANES), 1)
    o_ref[...] = jnp.where(lane < HEAD_DIM, outs[0],
                           pltpu.roll(outs[1], HEAD_DIM, axis=1)).astype(o_ref.dtype)


def _moba_attn(qa, ka, va, batch, seq):
    TB = MOBA_BLOCK
    nb = seq // TB
    HP = 2
    return pl.pallas_call(
        _moba_attn_kernel,
        grid=(batch, N_HEADS // HP, nb),
        in_specs=[pl.BlockSpec((None, HP, TB, LANES), lambda b, p, i: (b, p, i, 0)),
                  pl.BlockSpec((None, HP, seq, LANES), lambda b, p, i: (b, p, 0, 0)),
                  pl.BlockSpec((None, HP, seq, LANES), lambda b, p, i: (b, p, 0, 0))],
        out_specs=pl.BlockSpec((None, TB, LANES), lambda b, p, i: (b, i, p)),
        out_shape=jax.ShapeDtypeStruct((batch, seq, WIDTH), BF16),
        scratch_shapes=[pltpu.VMEM((HP, TB, LANES), F32), pltpu.VMEM((HP, TB, LANES), F32),
                        pltpu.VMEM((HP, TB, TB), F32), pltpu.VMEM((HP, TB, TB), F32)],
        compiler_params=_params("parallel", "parallel", "arbitrary"),
        name="moba_attn",
    )(qa, ka, va)


def _merge_kernel(x_ref, ya_ref, yb_ref, gate_ref, wa_ref, wb_ref, wo_ref, g2_ref, x1_ref, h2_ref):
    ua = _dot(ya_ref[...], wa_ref[...])
    ub = _dot(yb_ref[...], wb_ref[...])
    gate = gate_ref[...].astype(F32)
    mix = (gate[:, 0:D_MODEL] * ua + gate[:, D_MODEL:] * ub).astype(BF16)
    x1 = x_ref[...] + _dot(mix, wo_ref[...])
    x1_ref[...] = x1
    ms = jnp.mean(x1 * x1, axis=-1, keepdims=True)
    h2_ref[...] = (x1 * lax.rsqrt(ms + NORM_EPS) * g2_ref[...]).astype(BF16)


def _merge(x2, ya, yb, gates, wa, wb, wo, g2, tm):
    t = x2.shape[0]
    row = lambda n: pl.BlockSpec((tm, n), lambda i: (i, 0))
    return pl.pallas_call(
        _merge_kernel,
        grid=(t // tm,),
        in_specs=[row(D_MODEL), row(WIDTH), row(WIDTH), row(2 * D_MODEL), _const_spec(wa.shape),
                  _const_spec(wb.shape), _const_spec(wo.shape), _const_spec(g2.shape)],
        out_specs=[row(D_MODEL), row(D_MODEL)],
        out_shape=[jax.ShapeDtypeStruct((t, D_MODEL), F32), jax.ShapeDtypeStruct((t, D_MODEL), BF16)],
        compiler_params=_params("parallel"),
        name="merge",
    )(x2, ya, yb, gates, wa, wb, wo, g2)


FFN_HALO = 16
FFN_SPLIT = 2


def _ffn_kernel(h_ref, hp_ref, x1_ref, wua_ref, wub_ref, cw_ref, cb_ref, wd_ref, o_ref, *, tiles_per_seq):
    i = pl.program_id(0)
    tm = h_ref.shape[0]
    h = h_ref[...]
    halo = jnp.where(i % tiles_per_seq == 0, jnp.zeros_like(hp_ref[...]), hp_ref[...])
    h_ext = jnp.concatenate([halo, h], axis=0)
    fc = D_FF // FFN_SPLIT
    acc = x1_ref[...]
    for s in range(FFN_SPLIT):
        cs = slice(s * fc, (s + 1) * fc)
        a = _dot(h_ext, wua_ref[:, cs])
        b = _dot(h, wub_ref[:, cs])
        conv = (a[FFN_HALO:, :] * cw_ref[2:3, cs] + a[FFN_HALO - 1:FFN_HALO - 1 + tm, :] * cw_ref[1:2, cs]
                + a[FFN_HALO - 2:FFN_HALO - 2 + tm, :] * cw_ref[0:1, cs] + cb_ref[:, cs])
        gelu = 0.5 * conv * (1.0 + lax.erf(conv * (2.0 ** -0.5)))
        acc = acc + _dot((gelu * b).astype(BF16), wd_ref[cs, :])
    o_ref[...] = acc


def _ffn(h2, x1, wua, wub, cw, cb, wd, tm, seq):
    t = h2.shape[0]
    row = lambda n: pl.BlockSpec((tm, n), lambda i: (i, 0))
    halo = pl.BlockSpec((FFN_HALO, D_MODEL), lambda i: (jnp.maximum(i * (tm // FFN_HALO) - 1, 0), 0))
    return pl.pallas_call(
        functools.partial(_ffn_kernel, tiles_per_seq=seq // tm),
        grid=(t // tm,),
        in_specs=[row(D_MODEL), halo, row(D_MODEL), _const_spec(wua.shape), _const_spec(wub.shape),
                  _const_spec(cw.shape), _const_spec(cb.shape), _const_spec(wd.shape)],
        out_specs=row(D_MODEL),
        out_shape=jax.ShapeDtypeStruct((t, D_MODEL), F32),
        compiler_params=_params("parallel"),
        name="ffn",
    )(h2, h2, x1, wua, wub, cw, cb, wd)


def _pad_lora_cols(a):
    z = lambda n: jnp.zeros(a.shape[:-1] + (n,), a.dtype)
    o1, o2 = DECAY_LORA, DECAY_LORA + AAA_LORA
    return jnp.concatenate([a[..., :o1], z(LANES - DECAY_LORA), a[..., o1:o2], z(LANES - AAA_LORA),
                            a[..., o2:], z(2 * LANES - GATE_LORA)], axis=-1)


def _pad_rows(a, n):
    return jnp.concatenate([a, jnp.zeros((n - a.shape[0],) + a.shape[1:], a.dtype)], axis=0)


def kernel(x, positions, norm1_g, w_in, rwkv_mu, w_decay_up, decay_bias, w_aaa_up, aaa_bias, w_gate_up, rwkv_k_k, rwkv_k_a, rwkv_r_k, rwkv_ln_g, rwkv_ln_b, q_norm_g, k_norm_g, w_branch_a, w_branch_b, w_out, norm2_g, w_ffn_up, ffn_conv_w, ffn_conv_b, w_ffn_down):
    batch, seq, _ = x.shape
    depth = norm1_g.shape[0]
    assert seq % MOBA_BLOCK == 0 and seq % 512 == 0
    t = batch * seq
    row = lambda a: a.reshape(1, -1)
    c3 = 3 * WIDTH
    rwkv_in = c3 + DECAY_LORA + AAA_LORA + GATE_LORA
    hsum = _rwkv_masks()[1]

    x2 = x.reshape(t, D_MODEL)
    for l in range(depth):
        wi = w_in[l]
        wr = wi[:, :c3].astype(BF16)
        wl = _pad_lora_cols(wi[:, c3:rwkv_in]).astype(BF16)
        wm = wi[:, rwkv_in:rwkv_in + c3].astype(BF16)
        wg = wi[:, rwkv_in + c3:].astype(BF16)
        zr, zl, zm, gates = _inproj(x2, row(norm1_g[l]), wr, wl, wm, wg, tm=256)

        consts = [row(rwkv_mu[l][:c3]), _pad_lora_cols(row(rwkv_mu[l][c3:])),
                  _pad_rows(w_decay_up[l], LANES).astype(BF16), row(decay_bias[l]),
                  _pad_rows(w_aaa_up[l], LANES).astype(BF16), row(aaa_bias[l]),
                  _pad_rows(w_gate_up[l], 2 * LANES).astype(BF16),
                  row(rwkv_k_k[l]), row(rwkv_k_a[l]), row(rwkv_r_k[l]),
                  row(rwkv_ln_g[l]), row(rwkv_ln_b[l])]
        ya = _rwkv(zr, zl, consts, batch, seq).reshape(t, WIDTH)

        tile8 = lambda a: row(jnp.tile(a, N_HEADS))
        qa, ka, va = _moba_prep(zm, positions, tile8(q_norm_g[l]), tile8(k_norm_g[l]), hsum, batch, seq)
        yb = _moba_attn(qa, ka, va, batch, seq).reshape(t, WIDTH)

        x1, h2 = _merge(x2, ya, yb, gates, w_branch_a[l].astype(BF16), w_branch_b[l].astype(BF16),
                        w_out[l].astype(BF16), row(norm2_g[l]), tm=512)

        wu = w_ffn_up[l]
        x2 = _ffn(h2, x1, wu[:, :D_FF].astype(BF16), wu[:, D_FF:].astype(BF16), ffn_conv_w[l],
                  row(ffn_conv_b[l]), w_ffn_down[l].astype(BF16), tm=512, seq=seq)
    return x2.reshape(batch, seq, D_MODEL)
```

```python
import functools

import numpy as np
import jax
import jax.numpy as jnp
from jax import lax
from jax.experimental import pallas as pl
from jax.experimental.pallas import tpu as pltpu

F32 = jnp.float32
BF16 = jnp.bfloat16
HI = lax.Precision.HIGHEST

D_MODEL = 1024
HEAD_DIM = 64
N_HEADS = 8
WIDTH = N_HEADS * HEAD_DIM
DECAY_LORA = 64
AAA_LORA = 64
GATE_LORA = 160
LORA_PAD = 512
MOBA_BLOCK = 256
MOBA_TOPK = 3
ROT_DIM = HEAD_DIM // 4
ROPE_THETA = 500000.0
D_FF = 2816
NORM_EPS = 1e-6
GN_EPS = 64e-5
LANES = 128
CHUNK = 64
NEG_BIG = -32768.0
VMEM_LIMIT = 56 * 1024 * 1024


def _sigmoid(x):
    return 1.0 / (1.0 + jnp.exp(-x))


def _dot(a, b, precision=None):
    return jnp.dot(a, b, preferred_element_type=F32, precision=precision)


def _dot_nt(a, b, precision=None):
    return lax.dot_general(a, b, (((1,), (1,)), ((), ())), preferred_element_type=F32,
                           precision=precision)


def _dot_tn(a, b, precision=None):
    return lax.dot_general(a, b, (((0,), (0,)), ((), ())), preferred_element_type=F32,
                           precision=precision)


def _params(*sem):
    return pltpu.CompilerParams(dimension_semantics=sem, vmem_limit_bytes=VMEM_LIMIT)


def _const_spec(shape):
    nd = len(shape)
    return pl.BlockSpec(shape, lambda *_: (0,) * nd, pipeline_mode=pl.Buffered(1))


def _inproj_kernel(x_ref, g_ref, wr_ref, wl_ref, wm_ref, wg_ref,
                   zr_ref, zl_ref, zm_ref, gate_ref):
    half = x_ref.shape[0] // 2
    for rows in (slice(0, half), slice(half, 2 * half)):
        x = x_ref[rows, :]
        ms = jnp.mean(x * x, axis=-1, keepdims=True)
        h = (x * lax.rsqrt(ms + NORM_EPS) * g_ref[...]).astype(BF16)
        zr_ref[rows, :] = _dot(h, wr_ref[...])
        zl_ref[rows, :] = _dot(h, wl_ref[...])
        zm_ref[rows, :] = _dot(h, wm_ref[...])
        gate_ref[rows, :] = _sigmoid(_dot(h, wg_ref[...])).astype(BF16)


def _inproj(x2, g, wr, wl, wm, wg, tm):
    t = x2.shape[0]
    row = lambda n: pl.BlockSpec((tm, n), lambda i: (i, 0))
    return pl.pallas_call(
        _inproj_kernel,
        grid=(t // tm,),
        in_specs=[row(D_MODEL), _const_spec(g.shape), _const_spec(wr.shape), _const_spec(wl.shape),
                  _const_spec(wm.shape), _const_spec(wg.shape)],
        out_specs=[row(3 * WIDTH), row(LORA_PAD), row(3 * WIDTH), row(2 * D_MODEL)],
        out_shape=[jax.ShapeDtypeStruct((t, 3 * WIDTH), F32),
                   jax.ShapeDtypeStruct((t, LORA_PAD), F32),
                   jax.ShapeDtypeStruct((t, 3 * WIDTH), F32),
                   jax.ShapeDtypeStruct((t, 2 * D_MODEL), BF16)],
        compiler_params=_params("parallel"),
        name="inproj",
    )(x2, g, wr, wl, wm, wg)


RW_ROWS = 256
RW_GROUP = 256


def _rwkv_kernel(zr_ref, zrp_ref, zl_ref, zlp_ref, mur_ref, mul_ref, wd_ref, db_ref, wa_ref,
                 ab_ref, wg_ref, kk_ref, ka_ref, rk_ref, lng_ref, lnb_ref, cum_ref, hsum_ref, bd_ref,
                 o_ref, state_ref):
    c = pl.program_id(1)
    TL = zr_ref.shape[0]
    L = CHUNK
    G = RW_GROUP

    @pl.when(c == 0)
    def _():
        state_ref[...] = jnp.zeros_like(state_ref)

    def token_shift(z_ref, zp_ref, mu_ref):
        z = z_ref[...]
        prev = jnp.where(c == 0, 0.0, zp_ref[7:8, :])
        row = lax.broadcasted_iota(jnp.int32, z.shape, 0)
        zs = jnp.where(row == 0, prev, pltpu.roll(z, 1, axis=0))
        return z + mu_ref[...] * (zs - z)

    zr = token_shift(zr_ref, zrp_ref, mur_ref)
    zl = token_shift(zl_ref, zlp_ref, mul_ref)
    r = zr[:, 0:WIDTH]
    k = zr[:, WIDTH:2 * WIDTH]
    v = zr[:, 2 * WIDTH:3 * WIDTH]
    xw = zl[:, 0:LANES]
    xa = zl[:, LANES:2 * LANES]
    xg = zl[:, 2 * LANES:4 * LANES]

    dd = db_ref[...] + _dot(jnp.tanh(xw).astype(BF16), wd_ref[...])
    w_log = -(jnp.maximum(-dd, 0.0) + jnp.log(1.0 + jnp.exp(-jnp.abs(dd)))) - 0.5
    lw = -jnp.exp(w_log)
    asig = _sigmoid(ab_ref[...] + _dot(xa.astype(BF16), wa_ref[...]))
    g = _dot(_sigmoid(xg).astype(BF16), wg_ref[...])
    hsum = hsum_ref[...]
    kkf = k * kk_ref[...]
    kk = kkf * lax.rsqrt(jnp.maximum(_dot((kkf * kkf).astype(BF16), hsum), 1e-24))
    kmod = k * (1.0 + (asig - 1.0) * ka_ref[...])
    bonus = _dot((r * kmod * rk_ref[...]).astype(BF16), hsum) * v
    b = kk * asig

    lw_hi = lw.astype(BF16)
    lw_lo = (lw - lw_hi.astype(F32)).astype(BF16)
    cums = _dot(cum_ref[...], lw_hi) + _dot(cum_ref[...], lw_lo)
    cw = cums[0:TL]
    cw_end = cums[TL:2 * TL]
    e_neg = jnp.exp(-cw)
    e_end = jnp.exp(cw_end - cw)
    a_til = (-kk * jnp.exp(cw - lw)).astype(BF16)
    r_til = (r * jnp.exp(cw)).astype(BF16)
    b_til = (b * e_neg).astype(BF16)
    k_til = (kmod * e_neg).astype(BF16)
    b_hat = (b * e_end).astype(BF16)
    k_hat = (kmod * e_end).astype(BF16)
    w_end = jnp.exp(cw_end)
    vb = v.astype(BF16)

    bdm = bd_ref[...]

    def bd(x):
        return jnp.concatenate([x.astype(BF16)] * (G // L), axis=0) * bdm

    row2 = lax.broadcasted_iota(jnp.int32, (2 * L, G), 0)
    lane_t = lax.broadcasted_iota(jnp.int32, (2 * L, G), 1) % L
    tri2 = lane_t < jnp.where(row2 < L, row2, row2 - L + 1)
    eye = (lax.broadcasted_iota(jnp.int32, (L, G), 1) % L
           == lax.broadcasted_iota(jnp.int32, (L, G), 0)).astype(F32)
    bdmask = bdm > 0

    n_chunks, n_groups = TL // L, WIDTH // G
    pairs = [(ci, gi) for ci in range(n_chunks) for gi in range(n_groups)]

    def blk(x, cg):
        ci, gi = cg
        return x[ci * L:(ci + 1) * L, gi * G:(gi + 1) * G]

    p = {cg: jnp.concatenate([blk(a_til, cg), blk(r_til, cg)], axis=0) for cg in pairs}
    ab_rb = {cg: jnp.where(tri2, _dot_nt(p[cg], bd(blk(b_til, cg))), 0.0) for cg in pairs}
    ak_rk = {cg: jnp.where(tri2, _dot_nt(p[cg], bd(blk(k_til, cg))), 0.0) for cg in pairs}
    tinv = {cg: eye + ab_rb[cg][0:L] for cg in pairs}
    pw = {cg: _dot(ab_rb[cg][0:L].astype(BF16), bd(ab_rb[cg][0:L])) for cg in pairs}
    for _ in range(int(np.log2(L)) - 2):
        both = {cg: _dot(jnp.concatenate([pw[cg], tinv[cg]], axis=0).astype(BF16), bd(pw[cg])) for cg in pairs}
        pw = {cg: both[cg][0:L] for cg in pairs}
        tinv = {cg: tinv[cg] + both[cg][L:2 * L] for cg in pairs}
    tinv = {cg: (tinv[cg] + _dot(tinv[cg].astype(BF16), bd(pw[cg]))).astype(BF16) for cg in pairs}
    kv = {cg: _dot(ak_rk[cg].astype(BF16), bd(blk(vb, cg))) for cg in pairs}
    a_rb = {cg: ab_rb[cg][L:2 * L].astype(BF16) for cg in pairs}
    bk_t = {cg: jnp.concatenate([blk(b_hat, cg), blk(k_hat, cg)], axis=0).T.astype(BF16) for cg in pairs}
    w_col = {cg: jnp.concatenate([blk(w_end, cg)] * (G // L), axis=0).T for cg in pairs}

    st = [state_ref[gi] for gi in range(n_groups)]
    y_rows = []
    for ci in range(n_chunks):
        cgs = [(ci, gi) for gi in range(n_groups)]
        ps = [_dot(p[cg], st[cg[1]].astype(BF16)) for cg in cgs]
        u = [_dot(tinv[cg], bd(ps[gi][0:L] + kv[cg][0:L])) for gi, cg in enumerate(cgs)]
        upd = [_dot(bk_t[cg], jnp.concatenate([u[gi].astype(BF16), blk(vb, cg)], axis=0))
               for gi, cg in enumerate(cgs)]
        st = [w_col[cg] * st[gi] + jnp.where(bdmask, upd[gi], 0.0) for gi, cg in enumerate(cgs)]
        y_rows.append(jnp.concatenate(
            [ps[gi][L:2 * L] + kv[cg][L:2 * L] + _dot(a_rb[cg], bd(u[gi])) for gi, cg in enumerate(cgs)],
            axis=1))
    for gi in range(n_groups):
        state_ref[gi] = st[gi]
    y = jnp.concatenate(y_rows, axis=0)

    inv_n = 1.0 / HEAD_DIM
    mu = _dot(y.astype(BF16), hsum) * inv_n
    yc = y - mu
    var = _dot((yc * yc).astype(BF16), hsum) * inv_n
    yn = yc * lax.rsqrt(var + GN_EPS) * lng_ref[...] + lnb_ref[...]
    o_ref[...] = ((yn + bonus) * g).astype(o_ref.dtype)


def _rwkv_masks():
    t = np.arange(RW_ROWS)
    same_chunk = (t[:, None] // CHUNK) == (t[None, :] // CHUNK)
    tri = same_chunk & (t[:, None] >= t[None, :])
    cum = np.concatenate([tri, same_chunk], axis=0)
    c = np.arange(WIDTH)
    head_sum = (c[:, None] // HEAD_DIM) == (c[None, :] // HEAD_DIM)
    g = np.arange(RW_GROUP)
    bd = (g[:, None] // HEAD_DIM) == (g[None, :] // HEAD_DIM)
    return [jnp.asarray(m, BF16) for m in (cum, head_sum, bd)]


def _rwkv(zr, zl, consts, batch, seq):
    TL = RW_ROWS
    nt = seq // TL
    cur = lambda n: pl.BlockSpec((None, TL, n), lambda b, c: (b, c, 0))
    prv = lambda n: pl.BlockSpec((None, 8, n), lambda b, c: (b, jnp.maximum(c * (TL // 8) - 1, 0), 0))
    zr3 = zr.reshape(batch, seq, 3 * WIDTH)
    zl3 = zl.reshape(batch, seq, LORA_PAD)
    consts = list(consts) + _rwkv_masks()
    return pl.pallas_call(
        _rwkv_kernel,
        grid=(batch, nt),
        in_specs=[cur(3 * WIDTH), prv(3 * WIDTH), cur(LORA_PAD), prv(LORA_PAD)]
                 + [_const_spec(a.shape) for a in consts],
        out_specs=pl.BlockSpec((None, TL, WIDTH), lambda b, c: (b, c, 0)),
        out_shape=jax.ShapeDtypeStruct((batch, seq, WIDTH), BF16),
        scratch_shapes=[pltpu.VMEM((WIDTH // RW_GROUP, RW_GROUP, RW_GROUP), F32)],
        compiler_params=_params("parallel", "arbitrary"),
        name="rwkv",
    )(zr3, zr3, zl3, zl3, *consts)


QK_SCALE_LOG2 = HEAD_DIM ** -0.5 * float(np.log2(np.e))
AUX_BLOCKS = LANES // N_HEADS
HALF_HEADS = N_HEADS // 2


def _moba_prep_kernel(zm_ref, pos_ref, invf_ref, ropep_ref, qg_ref, kg_ref, hsum_ref,
                      q_out, k_out, v_out, km_ref, *, n_blocks):
    i = pl.program_id(1)
    TB = zm_ref.shape[0]

    @pl.when(i == 0)
    def _():
        km_ref[...] = jnp.zeros_like(km_ref)

    zm = zm_ref[...]
    ang = invf_ref[...] * pos_ref[...].astype(F32)
    cs = jnp.concatenate([jnp.cos(ang), jnp.sin(ang)], axis=0)
    cs_hi = cs.astype(BF16).astype(F32)
    tab = _dot_tn(jnp.concatenate([cs_hi, cs - cs_hi], axis=0), ropep_ref[...])
    lane = lax.broadcasted_iota(jnp.int32, (TB, LANES), 1)
    rep = WIDTH // LANES
    half = ROT_DIM // 2
    cos = jnp.concatenate([tab[:, 0:LANES] + jnp.where(lane % HEAD_DIM >= ROT_DIM, 1.0, 0.0)] * rep, axis=-1)
    sin_lo = jnp.concatenate([tab[:, LANES:2 * LANES]] * rep, axis=-1)
    sin_hi = jnp.concatenate([tab[:, 2 * LANES:3 * LANES]] * rep, axis=-1)
    hsum = hsum_ref[...]

    def norm_rope(t, gain):
        ms = _dot((t * t).astype(BF16), hsum) * (1.0 / HEAD_DIM)
        t = t * lax.rsqrt(ms + NORM_EPS) * gain
        return (t * cos + pltpu.roll(t, WIDTH - half, axis=1) * sin_lo
                + pltpu.roll(t, half, axis=1) * sin_hi)

    q = norm_rope(zm[:, 0:WIDTH], qg_ref[...])
    k = norm_rope(zm[:, WIDTH:2 * WIDTH], kg_ref[...])
    v = zm[:, 2 * WIDTH:3 * WIDTH]

    km = km_ref[...]
    gate = _dot_nt(km, q, HI).reshape(N_HEADS, AUX_BLOCKS, TB)
    n_idx = lax.broadcasted_iota(jnp.int32, gate.shape, 1)
    gsel = jnp.where(n_idx < i, gate, -jnp.inf)
    picked = n_idx < 0
    for _ in range(MOBA_TOPK):
        m = jnp.max(gsel, axis=1, keepdims=True)
        idx = jnp.min(jnp.where(gsel == m, n_idx, AUX_BLOCKS), axis=1, keepdims=True)
        pick = n_idx == idx
        picked = picked | (pick & (m > -jnp.inf))
        gsel = jnp.where(pick, -jnp.inf, gsel)
    keep = picked | (n_idx == i) | (n_idx >= n_blocks)
    bias = jnp.where(keep, 0.0, NEG_BIG).reshape(LANES, TB).T
    bias_by_half = [pltpu.roll(bias, HEAD_DIM, axis=1), bias]

    rowk = lax.broadcasted_iota(jnp.int32, km.shape, 0)
    lanek = lax.broadcasted_iota(jnp.int32, km.shape, 1)
    mine = (rowk % AUX_BLOCKS == i) & (lanek // HEAD_DIM == rowk // AUX_BLOCKS)
    km_ref[...] = jnp.where(mine, jnp.mean(k, axis=0, keepdims=True), km)

    is_data = lane < HEAD_DIM

    for h in range(N_HEADS):
        ps = slice((h // 2) * LANES, (h // 2 + 1) * LANES)

        def head_base(t):
            p = t[:, ps]
            return p if h % 2 == 0 else pltpu.roll(p, HEAD_DIM, axis=1)

        qb, kb, vb = head_base(q), head_base(k), head_base(v)
        own_lane = HEAD_DIM + (h % HALF_HEADS) * AUX_BLOCKS + i
        q_out[h] = jnp.where(is_data, qb * QK_SCALE_LOG2, bias_by_half[h // HALF_HEADS]).astype(BF16)
        k_out[h] = jnp.where(is_data, kb, jnp.where(lane == own_lane, 1.0, 0.0)).astype(BF16)
        v_out[h] = jnp.where(is_data, vb, jnp.where(lane == HEAD_DIM, 1.0, 0.0)).T.astype(BF16)


def _rope_tables():
    half = ROT_DIM // 2
    invf = ROPE_THETA ** (-np.arange(half, dtype=np.float64) / half)
    lane = np.arange(LANES) % HEAD_DIM
    f = np.arange(half)[:, None]
    p_cos = ((lane[None, :] < ROT_DIM) & (lane[None, :] % half == f)).astype(np.float64)
    p_lo = -(lane[None, :] == f).astype(np.float64)
    p_hi = (lane[None, :] == f + half).astype(np.float64)
    z = np.zeros_like(p_cos)
    cos_rows = np.concatenate([p_cos, z, z], axis=1)
    sin_rows = np.concatenate([z, p_lo, p_hi], axis=1)
    expand = np.concatenate([cos_rows, sin_rows, cos_rows, sin_rows], axis=0)
    return (jnp.asarray(np.broadcast_to(invf[:, None], (half, MOBA_BLOCK)), F32), jnp.asarray(expand, F32))


def _moba_prep(zm, positions, qg, kg, hsum, batch, seq):
    TB = MOBA_BLOCK
    nb = seq // TB
    assert nb <= AUX_BLOCKS
    zm3 = zm.reshape(batch, seq, 3 * WIDTH)
    pos4 = positions.reshape(batch, nb, 1, TB)
    invf, expand = _rope_tables()
    aug = pl.BlockSpec((None, N_HEADS, TB, LANES), lambda b, i: (b, 0, i, 0))
    aug_shape = jax.ShapeDtypeStruct((batch, N_HEADS, seq, LANES), BF16)
    return pl.pallas_call(
        functools.partial(_moba_prep_kernel, n_blocks=nb),
        grid=(batch, nb),
        in_specs=[pl.BlockSpec((None, TB, 3 * WIDTH), lambda b, i: (b, i, 0)),
                  pl.BlockSpec((None, None, 1, TB), lambda b, i: (b, i, 0, 0)),
                  _const_spec(invf.shape), _const_spec(expand.shape), _const_spec(qg.shape),
                  _const_spec(kg.shape), _const_spec(hsum.shape)],
        out_specs=[aug, aug, pl.BlockSpec((None, N_HEADS, None, LANES, TB), lambda b, i: (b, 0, i, 0, 0))],
        out_shape=[aug_shape, aug_shape, jax.ShapeDtypeStruct((batch, N_HEADS, nb, LANES, TB), BF16)],
        scratch_shapes=[pltpu.VMEM((LANES, WIDTH), F32)],
        compiler_params=_params("parallel", "arbitrary"),
        name="moba_prep",
    )(zm3, pos4, invf, expand, qg, kg, hsum)


def _moba_attn_kernel(q_ref, k_ref, v_ref, o_ref, acc_ref, m_ref, s_ref, p_ref, a_ref):
    i = pl.program_id(2)
    TB = q_ref.shape[1]
    HP = q_ref.shape[0]
    n_steps = i + 1

    def block_of(tau):
        return jnp.where(tau == 0, i, jnp.minimum(tau - 1, jnp.maximum(i - 1, 0)))

    def qk(tau, par, masked=False):
        koff = pl.multiple_of(block_of(tau) * TB, TB)
        for hh in range(HP):
            s = _dot_nt(k_ref[hh, pl.ds(koff, TB), :], q_ref[hh])
            if masked:
                key = lax.broadcasted_iota(jnp.int32, (TB, TB), 0)
                qry = lax.broadcasted_iota(jnp.int32, (TB, TB), 1)
                s = jnp.where(key <= qry, s, -1e30)
            s_ref[par, hh] = s

    def softmax(par):
        for hh in range(HP):
            s = s_ref[par, hh]
            m_old = m_ref[hh]
            m_new = jnp.maximum(m_old, jnp.max(s, axis=0, keepdims=True))
            m_ref[hh] = m_new
            a_ref[par, hh] = jnp.exp2(m_old - m_new)
            p_ref[par, hh] = jnp.exp2(s - m_new).astype(BF16)

    def pv(tau, par):
        live = (tau < n_steps).astype(BF16)
        blk = block_of(tau)
        for hh in range(HP):
            rescale = jnp.where(tau < n_steps, a_ref[par, hh], 1.0)
            acc_ref[hh] = acc_ref[hh] * rescale + _dot(v_ref[hh, blk] * live, p_ref[par, hh])

    for hh in range(HP):
        m_ref[hh] = jnp.full((1, TB), -1e30, F32)
        acc_ref[hh] = jnp.zeros((LANES, TB), F32)
    qk(0, 0, masked=True)
    qk(1, 1)
    softmax(0)

    def slot_pair(pp, carry):
        t = 2 * pp
        qk(t + 2, 0)
        pv(t, 0)
        softmax(1)
        qk(t + 3, 1)
        pv(t + 1, 1)
        softmax(0)
        return carry

    lax.fori_loop(0, (n_steps + 1) // 2, slot_pair, 0)

    outs = []
    for hh in range(HP):
        acc = acc_ref[hh]
        outs.append((acc / acc[HEAD_DIM:HEAD_DIM + 1, :]).T)
    lane = lax.broadcasted_iota(jnp.int32, (TB, LANES), 1)
    for pr in range(HP // 2):
        o_ref[:, pr * LANES:(pr + 1) * LANES] = jnp.where(
            lane < HEAD_DIM, outs[2 * pr], pltpu.roll(outs[2 * pr + 1], HEAD_DIM, axis=1)).astype(o_ref.dtype)


ATTN_HEADS = 8


def _moba_attn(qa, ka, va, batch, seq):
    TB = MOBA_BLOCK
    nb = seq // TB
    HP = ATTN_HEADS
    return pl.pallas_call(
        _moba_attn_kernel,
        grid=(batch, N_HEADS // HP, nb),
        in_specs=[pl.BlockSpec((None, HP, TB, LANES), lambda b, p, i: (b, p, i, 0)),
                  pl.BlockSpec((None, HP, seq, LANES), lambda b, p, i: (b, p, 0, 0)),
                  pl.BlockSpec((None, HP, nb, LANES, TB), lambda b, p, i: (b, p, 0, 0, 0))],
        out_specs=pl.BlockSpec((None, TB, HP * HEAD_DIM), lambda b, p, i: (b, i, p)),
        out_shape=jax.ShapeDtypeStruct((batch, seq, WIDTH), BF16),
        scratch_shapes=[pltpu.VMEM((HP, LANES, TB), F32), pltpu.VMEM((HP, 1, TB), F32),
                        pltpu.VMEM((2, HP, TB, TB), F32), pltpu.VMEM((2, HP, TB, TB), BF16),
                        pltpu.VMEM((2, HP, 1, TB), F32)],
        compiler_params=_params("parallel", "parallel", "arbitrary"),
        name="moba_attn",
    )(qa, ka, va)


def _merge_kernel(x_ref, ya_ref, yb_ref, gate_ref, wa_ref, wb_ref, wo_ref, g2_ref, x1_ref, h2_ref):
    half = x_ref.shape[0] // 2
    halves = (slice(0, half), slice(half, 2 * half))
    ua = [_dot(ya_ref[rows, :], wa_ref[...]) for rows in halves]
    ub = [_dot(yb_ref[rows, :], wb_ref[...]) for rows in halves]
    for hi, rows in enumerate(halves):
        gate = gate_ref[rows, :].astype(F32)
        mix = (gate[:, 0:D_MODEL] * ua[hi] + gate[:, D_MODEL:] * ub[hi]).astype(BF16)
        x1 = x_ref[rows, :] + _dot(mix, wo_ref[...])
        x1_ref[rows, :] = x1
        ms = jnp.mean(x1 * x1, axis=-1, keepdims=True)
        h2_ref[rows, :] = (x1 * lax.rsqrt(ms + NORM_EPS) * g2_ref[...]).astype(BF16)


def _merge(x2, ya, yb, gates, wa, wb, wo, g2, tm):
    t = x2.shape[0]
    row = lambda n: pl.BlockSpec((tm, n), lambda i: (i, 0))
    return pl.pallas_call(
        _merge_kernel,
        grid=(t // tm,),
        in_specs=[row(D_MODEL), row(WIDTH), row(WIDTH), row(2 * D_MODEL), _const_spec(wa.shape),
                  _const_spec(wb.shape), _const_spec(wo.shape), _const_spec(g2.shape)],
        out_specs=[row(D_MODEL), row(D_MODEL)],
        out_shape=[jax.ShapeDtypeStruct((t, D_MODEL), F32), jax.ShapeDtypeStruct((t, D_MODEL), BF16)],
        compiler_params=_params("parallel"),
        name="merge",
    )(x2, ya, yb, gates, wa, wb, wo, g2)


FFN_HALO = 16
FFN_COLS = (768, 768, 768, 512)


def _ffn_kernel(h_ref, hp_ref, x1_ref, wua_ref, wub_ref, cw_ref, cb_ref, wd_ref, o_ref, *, tiles_per_seq):
    i = pl.program_id(0)
    tm = h_ref.shape[0]
    h = h_ref[...]
    halo = jnp.where(i % tiles_per_seq == 0, jnp.zeros_like(hp_ref[...]), hp_ref[...])
    h_ext = jnp.concatenate([halo, h], axis=0)
    starts = np.cumsum((0,) + FFN_COLS)
    groups = [slice(int(starts[g]), int(starts[g + 1])) for g in range(len(FFN_COLS))]

    def up(cs):
        return _dot(h_ext, wua_ref[:, cs]), _dot(h, wub_ref[:, cs])

    acc = x1_ref[...]
    nxt = up(groups[0])
    for g, cs in enumerate(groups):
        a, b = nxt
        if g + 1 < len(groups):
            nxt = up(groups[g + 1])
        conv = (a[FFN_HALO:, :] * cw_ref[2:3, cs] + a[FFN_HALO - 1:FFN_HALO - 1 + tm, :] * cw_ref[1:2, cs]
                + a[FFN_HALO - 2:FFN_HALO - 2 + tm, :] * cw_ref[0:1, cs] + cb_ref[:, cs])
        gelu = 0.5 * conv * (1.0 + lax.erf(conv * (2.0 ** -0.5)))
        acc = acc + _dot((gelu * b).astype(BF16), wd_ref[cs, :])
    o_ref[...] = acc


def _ffn(h2, x1, wua, wub, cw, cb, wd, tm, seq):
    t = h2.shape[0]
    row = lambda n: pl.BlockSpec((tm, n), lambda i: (i, 0))
    halo = pl.BlockSpec((FFN_HALO, D_MODEL), lambda i: (jnp.maximum(i * (tm // FFN_HALO) - 1, 0), 0))
    return pl.pallas_call(
        functools.partial(_ffn_kernel, tiles_per_seq=seq // tm),
        grid=(t // tm,),
        in_specs=[row(D_MODEL), halo, row(D_MODEL), _const_spec(wua.shape), _const_spec(wub.shape),
                  _const_spec(cw.shape), _const_spec(cb.shape), _const_spec(wd.shape)],
        out_specs=row(D_MODEL),
        out_shape=jax.ShapeDtypeStruct((t, D_MODEL), F32),
        compiler_params=_params("parallel"),
        name="ffn",
    )(h2, h2, x1, wua, wub, cw, cb, wd)


def _pad_lora_cols(a):
    z = lambda n: jnp.zeros(a.shape[:-1] + (n,), a.dtype)
    o1, o2 = DECAY_LORA, DECAY_LORA + AAA_LORA
    return jnp.concatenate([a[..., :o1], z(LANES - DECAY_LORA), a[..., o1:o2], z(LANES - AAA_LORA),
                            a[..., o2:], z(2 * LANES - GATE_LORA)], axis=-1)


def _pad_rows(a, n):
    return jnp.concatenate([a, jnp.zeros((n - a.shape[0],) + a.shape[1:], a.dtype)], axis=0)


def kernel(x, positions, norm1_g, w_in, rwkv_mu, w_decay_up, decay_bias, w_aaa_up, aaa_bias, w_gate_up, rwkv_k_k, rwkv_k_a, rwkv_r_k, rwkv_ln_g, rwkv_ln_b, q_norm_g, k_norm_g, w_branch_a, w_branch_b, w_out, norm2_g, w_ffn_up, ffn_conv_w, ffn_conv_b, w_ffn_down):
    batch, seq, _ = x.shape
    depth = norm1_g.shape[0]
    assert seq % MOBA_BLOCK == 0 and seq % 512 == 0
    t = batch * seq
    row = lambda a: a.reshape(1, -1)
    c3 = 3 * WIDTH
    rwkv_in = c3 + DECAY_LORA + AAA_LORA + GATE_LORA
    hsum = _rwkv_masks()[1]

    x2 = x.reshape(t, D_MODEL)
    for l in range(depth):
        wi = w_in[l]
        wr = wi[:, :c3].astype(BF16)
        wl = _pad_lora_cols(wi[:, c3:rwkv_in]).astype(BF16)
        wm = wi[:, rwkv_in:rwkv_in + c3].astype(BF16)
        wg = wi[:, rwkv_in + c3:].astype(BF16)
        zr, zl, zm, gates = _inproj(x2, row(norm1_g[l]), wr, wl, wm, wg, tm=512)

        consts = [row(rwkv_mu[l][:c3]), _pad_lora_cols(row(rwkv_mu[l][c3:])),
                  _pad_rows(w_decay_up[l], LANES).astype(BF16), row(decay_bias[l]),
                  _pad_rows(w_aaa_up[l], LANES).astype(BF16), row(aaa_bias[l]),
                  _pad_rows(w_gate_up[l], 2 * LANES).astype(BF16),
                  row(rwkv_k_k[l]), row(rwkv_k_a[l]), row(rwkv_r_k[l]),
                  row(rwkv_ln_g[l]), row(rwkv_ln_b[l])]
        ya = _rwkv(zr, zl, consts, batch, seq).reshape(t, WIDTH)

        tile8 = lambda a: row(jnp.tile(a, N_HEADS))
        qa, ka, va = _moba_prep(zm, positions, tile8(q_norm_g[l]), tile8(k_norm_g[l]), hsum, batch, seq)
        yb = _moba_attn(qa, ka, va, batch, seq).reshape(t, WIDTH)

        x1, h2 = _merge(x2, ya, yb, gates, w_branch_a[l].astype(BF16), w_branch_b[l].astype(BF16),
                        w_out[l].astype(BF16), row(norm2_g[l]), tm=512)

        wu = w_ffn_up[l]
        x2 = _ffn(h2, x1, wu[:, :D_FF].astype(BF16), wu[:, D_FF:].astype(BF16), ffn_conv_w[l],
                  row(ffn_conv_b[l]), w_ffn_down[l].astype(BF16), tm=512, seq=seq)
    return x2.reshape(batch, seq, D_MODEL)
```

```python
import functools

import numpy as np
import jax
import jax.numpy as jnp
from jax import lax
from jax.experimental import pallas as pl
from jax.experimental.pallas import tpu as pltpu

F32 = jnp.float32
BF16 = jnp.bfloat16
HI = lax.Precision.HIGHEST

D_MODEL = 1024
HEAD_DIM = 64
N_HEADS = 8
WIDTH = N_HEADS * HEAD_DIM
DECAY_LORA = 64
AAA_LORA = 64
GATE_LORA = 160
LORA_PAD = 512
MOBA_BLOCK = 256
MOBA_TOPK = 3
ROT_DIM = HEAD_DIM // 4
ROPE_THETA = 500000.0
D_FF = 2816
NORM_EPS = 1e-6
GN_EPS = 64e-5
LANES = 128
CHUNK = 64
NEG_BIG = -32768.0
VMEM_LIMIT = 56 * 1024 * 1024


def _sigmoid(x):
    return 1.0 / (1.0 + jnp.exp(-x))


def _dot(a, b, precision=None):
    return jnp.dot(a, b, preferred_element_type=F32, precision=precision)


def _dot_nt(a, b, precision=None):
    return lax.dot_general(a, b, (((1,), (1,)), ((), ())), preferred_element_type=F32,
                           precision=precision)


def _head_sums(x, bd_ones):
    g = bd_ones.shape[0]
    return jnp.concatenate([_dot(x[:, j:j + g].astype(BF16), bd_ones) for j in range(0, x.shape[1], g)],
                           axis=1)


def _dot_tn(a, b, precision=None):
    return lax.dot_general(a, b, (((0,), (0,)), ((), ())), preferred_element_type=F32,
                           precision=precision)


def _params(*sem):
    return pltpu.CompilerParams(dimension_semantics=sem, vmem_limit_bytes=VMEM_LIMIT)


def _const_spec(shape):
    nd = len(shape)
    return pl.BlockSpec(shape, lambda *_: (0,) * nd, pipeline_mode=pl.Buffered(1))


def _inproj_kernel(x_ref, g_ref, wr_ref, wl_ref, wm_ref, wg_ref,
                   zr_ref, zl_ref, zm_ref, gate_ref):
    half = x_ref.shape[0] // 2
    for rows in (slice(0, half), slice(half, 2 * half)):
        x = x_ref[rows, :]
        ms = jnp.mean(x * x, axis=-1, keepdims=True)
        h = (x * lax.rsqrt(ms + NORM_EPS) * g_ref[...]).astype(BF16)
        zr_ref[rows, :] = _dot(h, wr_ref[...])
        zl_ref[rows, :] = _dot(h, wl_ref[...])
        zm_ref[rows, :] = _dot(h, wm_ref[...])
        gate_ref[rows, :] = _sigmoid(_dot(h, wg_ref[...])).astype(BF16)


def _inproj(x2, g, wr, wl, wm, wg, tm):
    t = x2.shape[0]
    row = lambda n: pl.BlockSpec((tm, n), lambda i: (i, 0))
    return pl.pallas_call(
        _inproj_kernel,
        grid=(t // tm,),
        in_specs=[row(D_MODEL), _const_spec(g.shape), _const_spec(wr.shape), _const_spec(wl.shape),
                  _const_spec(wm.shape), _const_spec(wg.shape)],
        out_specs=[row(3 * WIDTH), row(LORA_PAD), row(3 * WIDTH), row(2 * D_MODEL)],
        out_shape=[jax.ShapeDtypeStruct((t, 3 * WIDTH), F32),
                   jax.ShapeDtypeStruct((t, LORA_PAD), F32),
                   jax.ShapeDtypeStruct((t, 3 * WIDTH), F32),
                   jax.ShapeDtypeStruct((t, 2 * D_MODEL), BF16)],
        compiler_params=_params("parallel"),
        name="inproj",
    )(x2, g, wr, wl, wm, wg)


RW_ROWS = 256
RW_SEQS = 4
RW_GROUP = 256


def _rwkv_kernel(zr_ref, zrp_ref, zl_ref, zlp_ref, mur_ref, mul_ref, wd_ref, db_ref, wa_ref,
                 ab_ref, wg_ref, kk_ref, ka_ref, rk_ref, lng_ref, lnb_ref, cum_ref, bd_ref,
                 o_ref, state_ref):
    c = pl.program_id(1)
    n_seq, TS = zr_ref.shape[0], zr_ref.shape[1]
    TL = n_seq * TS
    L = CHUNK
    G = RW_GROUP

    @pl.when(c == 0)
    def _():
        state_ref[...] = jnp.zeros_like(state_ref)

    def token_shift(z_ref, zp_ref, mu_ref):
        z = z_ref[...].reshape(TL, z_ref.shape[2])
        row = lax.broadcasted_iota(jnp.int32, z.shape, 0)
        zs = pltpu.roll(z, 1, axis=0)
        for s in range(n_seq):
            zs = jnp.where(row == s * TS, jnp.where(c == 0, 0.0, zp_ref[s, 7:8, :]), zs)
        return z + mu_ref[...] * (zs - z)

    zr = token_shift(zr_ref, zrp_ref, mur_ref)
    zl = token_shift(zl_ref, zlp_ref, mul_ref)
    r = zr[:, 0:WIDTH]
    k = zr[:, WIDTH:2 * WIDTH]
    v = zr[:, 2 * WIDTH:3 * WIDTH]
    xw = zl[:, 0:LANES]
    xa = zl[:, LANES:2 * LANES]
    xg = zl[:, 2 * LANES:4 * LANES]

    dd = db_ref[...] + _dot(jnp.tanh(xw).astype(BF16), wd_ref[...])
    w_log = -(jnp.maximum(-dd, 0.0) + jnp.log(1.0 + jnp.exp(-jnp.abs(dd)))) - 0.5
    lw = -jnp.exp(w_log)
    asig = _sigmoid(ab_ref[...] + _dot(xa.astype(BF16), wa_ref[...]))
    g = _dot(_sigmoid(xg).astype(BF16), wg_ref[...])
    bdm = bd_ref[...]
    kkf = k * kk_ref[...]
    kk = kkf * lax.rsqrt(jnp.maximum(_head_sums(kkf * kkf, bdm), 1e-24))
    kmod = k * (1.0 + (asig - 1.0) * ka_ref[...])
    bonus = _head_sums(r * kmod * rk_ref[...], bdm) * v
    b = kk * asig

    lw_hi = lw.astype(BF16)
    lw_lo = (lw - lw_hi.astype(F32)).astype(BF16)
    cum = cum_ref[...]
    cums = [_dot(cum, lw_hi[ci * L:(ci + 1) * L]) + _dot(cum, lw_lo[ci * L:(ci + 1) * L]) for ci in range(TL // L)]
    cw = jnp.concatenate([cs[0:L] for cs in cums], axis=0)
    cw_end = jnp.concatenate([cs[L:2 * L] for cs in cums], axis=0)
    e_neg = jnp.exp(-cw)
    e_end = jnp.exp(cw_end - cw)
    a_til = (-kk * jnp.exp(cw - lw)).astype(BF16)
    r_til = (r * jnp.exp(cw)).astype(BF16)
    b_til = (b * e_neg).astype(BF16)
    k_til = (kmod * e_neg).astype(BF16)
    b_hat = (b * e_end).astype(BF16)
    k_hat = (kmod * e_end).astype(BF16)
    w_end = jnp.exp(cw_end)
    vb = v.astype(BF16)

    def bd(x):
        return jnp.concatenate([x.astype(BF16)] * (G // L), axis=0) * bdm

    row2 = lax.broadcasted_iota(jnp.int32, (2 * L, G), 0)
    lane_t = lax.broadcasted_iota(jnp.int32, (2 * L, G), 1) % L
    tri2 = lane_t < jnp.where(row2 < L, row2, row2 - L + 1)
    eye = (lax.broadcasted_iota(jnp.int32, (L, G), 1) % L
           == lax.broadcasted_iota(jnp.int32, (L, G), 0)).astype(F32)
    bdmask = bdm > 0

    n_chunks, n_groups = TL // L, WIDTH // G
    pairs = [(ci, gi) for ci in range(n_chunks) for gi in range(n_groups)]

    def blk(x, cg):
        ci, gi = cg
        return x[ci * L:(ci + 1) * L, gi * G:(gi + 1) * G]

    p = {cg: jnp.concatenate([blk(a_til, cg), blk(r_til, cg)], axis=0) for cg in pairs}
    ab_rb = {cg: jnp.where(tri2, _dot_nt(p[cg], bd(blk(b_til, cg))), 0.0) for cg in pairs}
    ak_rk = {cg: jnp.where(tri2, _dot_nt(p[cg], bd(blk(k_til, cg))), 0.0) for cg in pairs}
    tinv = {cg: eye + ab_rb[cg][0:L] for cg in pairs}
    pw = {cg: _dot(ab_rb[cg][0:L].astype(BF16), bd(ab_rb[cg][0:L])) for cg in pairs}
    for _ in range(int(np.log2(L)) - 2):
        both = {cg: _dot(jnp.concatenate([pw[cg], tinv[cg]], axis=0).astype(BF16), bd(pw[cg])) for cg in pairs}
        pw = {cg: both[cg][0:L] for cg in pairs}
        tinv = {cg: tinv[cg] + both[cg][L:2 * L] for cg in pairs}
    tinv = {cg: (tinv[cg] + _dot(tinv[cg].astype(BF16), bd(pw[cg]))).astype(BF16) for cg in pairs}
    kv = {cg: _dot(ak_rk[cg].astype(BF16), bd(blk(vb, cg))) for cg in pairs}
    a_rb = {cg: ab_rb[cg][L:2 * L].astype(BF16) for cg in pairs}
    bk_t = {cg: jnp.concatenate([blk(b_hat, cg), blk(k_hat, cg)], axis=0).T.astype(BF16) for cg in pairs}
    w_col = {cg: jnp.concatenate([blk(w_end, cg)] * (G // L), axis=0).T for cg in pairs}

    chains = [(s, gi) for s in range(n_seq) for gi in range(n_groups)]
    seq_chunks = TS // L
    st = {sg: state_ref[sg[0] * n_groups + sg[1]] for sg in chains}
    y_blk = {}
    for ci in range(seq_chunks):
        cg = {sg: (sg[0] * seq_chunks + ci, sg[1]) for sg in chains}
        ps = {sg: _dot(p[cg[sg]], st[sg].astype(BF16)) for sg in chains}
        u = {sg: _dot(tinv[cg[sg]], bd(ps[sg][0:L] + kv[cg[sg]][0:L])) for sg in chains}
        upd = {sg: _dot(bk_t[cg[sg]], jnp.concatenate([u[sg].astype(BF16), blk(vb, cg[sg])], axis=0))
               for sg in chains}
        st = {sg: w_col[cg[sg]] * st[sg] + jnp.where(bdmask, upd[sg], 0.0) for sg in chains}
        for sg in chains:
            y_blk[cg[sg]] = ps[sg][L:2 * L] + kv[cg[sg]][L:2 * L] + _dot(a_rb[cg[sg]], bd(u[sg]))
    for sg in chains:
        state_ref[sg[0] * n_groups + sg[1]] = st[sg]
    y = jnp.concatenate([jnp.concatenate([y_blk[(ci, gi)] for gi in range(n_groups)], axis=1)
                         for ci in range(n_chunks)], axis=0)

    inv_n = 1.0 / HEAD_DIM
    mu = _head_sums(y, bdm) * inv_n
    yc = y - mu
    var = _head_sums(yc * yc, bdm) * inv_n
    yn = yc * lax.rsqrt(var + GN_EPS) * lng_ref[...] + lnb_ref[...]
    o_ref[...] = ((yn + bonus) * g).astype(o_ref.dtype).reshape(o_ref.shape)


def _rwkv_masks():
    t = np.arange(CHUNK)
    tri = t[:, None] >= t[None, :]
    cum = np.concatenate([tri, np.ones_like(tri)], axis=0)
    g = np.arange(RW_GROUP)
    bd = (g[:, None] // HEAD_DIM) == (g[None, :] // HEAD_DIM)
    return [jnp.asarray(m, BF16) for m in (cum, bd)]


def _rwkv(zr, zl, consts, batch, seq):
    TL = RW_ROWS
    nt = seq // TL
    ns = RW_SEQS if batch % RW_SEQS == 0 else 1
    cur = lambda n: pl.BlockSpec((ns, TL, n), lambda b, c: (b, c, 0))
    prv = lambda n: pl.BlockSpec((ns, 8, n), lambda b, c: (b, jnp.maximum(c * (TL // 8) - 1, 0), 0))
    zr3 = zr.reshape(batch, seq, 3 * WIDTH)
    zl3 = zl.reshape(batch, seq, LORA_PAD)
    consts = list(consts) + _rwkv_masks()
    return pl.pallas_call(
        _rwkv_kernel,
        grid=(batch // ns, nt),
        in_specs=[cur(3 * WIDTH), prv(3 * WIDTH), cur(LORA_PAD), prv(LORA_PAD)]
                 + [_const_spec(a.shape) for a in consts],
        out_specs=pl.BlockSpec((ns, TL, WIDTH), lambda b, c: (b, c, 0)),
        out_shape=jax.ShapeDtypeStruct((batch, seq, WIDTH), BF16),
        scratch_shapes=[pltpu.VMEM((ns * (WIDTH // RW_GROUP), RW_GROUP, RW_GROUP), F32)],
        compiler_params=_params("parallel", "arbitrary"),
        name="rwkv",
    )(zr3, zr3, zl3, zl3, *consts)


QK_SCALE_LOG2 = HEAD_DIM ** -0.5 * float(np.log2(np.e))
AUX_BLOCKS = LANES // N_HEADS
HALF_HEADS = N_HEADS // 2


def _moba_prep_kernel(zm_ref, pos_ref, invf_ref, ropep_ref, qg_ref, kg_ref, hsum_ref,
                      q_out, k_out, v_out, km_ref, *, n_blocks):
    i = pl.program_id(1)
    TB = zm_ref.shape[0]

    @pl.when(i == 0)
    def _():
        km_ref[...] = jnp.zeros_like(km_ref)

    zm = zm_ref[...]
    ang = invf_ref[...] * pos_ref[...].astype(F32)
    cs = jnp.concatenate([jnp.cos(ang), jnp.sin(ang)], axis=0)
    cs_hi = cs.astype(BF16).astype(F32)
    tab = _dot_tn(jnp.concatenate([cs_hi, cs - cs_hi], axis=0), ropep_ref[...])
    lane = lax.broadcasted_iota(jnp.int32, (TB, LANES), 1)
    rep = WIDTH // LANES
    half = ROT_DIM // 2
    cos = jnp.concatenate([tab[:, 0:LANES] + jnp.where(lane % HEAD_DIM >= ROT_DIM, 1.0, 0.0)] * rep, axis=-1)
    sin_lo = jnp.concatenate([tab[:, LANES:2 * LANES]] * rep, axis=-1)
    sin_hi = jnp.concatenate([tab[:, 2 * LANES:3 * LANES]] * rep, axis=-1)
    hsum = hsum_ref[...]

    def norm_rope(t, gain):
        ms = _head_sums(t * t, hsum) * (1.0 / HEAD_DIM)
        t = t * lax.rsqrt(ms + NORM_EPS) * gain
        return (t * cos + pltpu.roll(t, WIDTH - half, axis=1) * sin_lo
                + pltpu.roll(t, half, axis=1) * sin_hi)

    q = norm_rope(zm[:, 0:WIDTH], qg_ref[...])
    k = norm_rope(zm[:, WIDTH:2 * WIDTH], kg_ref[...])
    v = zm[:, 2 * WIDTH:3 * WIDTH]

    km = km_ref[...]
    gate = _dot_nt(km, q, HI).reshape(N_HEADS, AUX_BLOCKS, TB)
    n_idx = lax.broadcasted_iota(jnp.int32, gate.shape, 1)
    gsel = jnp.where(n_idx < i, gate, -jnp.inf)
    picked = n_idx < 0
    for _ in range(MOBA_TOPK):
        m = jnp.max(gsel, axis=1, keepdims=True)
        idx = jnp.min(jnp.where(gsel == m, n_idx, AUX_BLOCKS), axis=1, keepdims=True)
        pick = n_idx == idx
        picked = picked | (pick & (m > -jnp.inf))
        gsel = jnp.where(pick, -jnp.inf, gsel)
    keep = picked | (n_idx == i) | (n_idx >= n_blocks)
    bias = jnp.where(keep, 0.0, NEG_BIG).reshape(LANES, TB).T
    bias_by_half = [pltpu.roll(bias, HEAD_DIM, axis=1), bias]

    rowk = lax.broadcasted_iota(jnp.int32, km.shape, 0)
    lanek = lax.broadcasted_iota(jnp.int32, km.shape, 1)
    mine = (rowk % AUX_BLOCKS == i) & (lanek // HEAD_DIM == rowk // AUX_BLOCKS)
    km_ref[...] = jnp.where(mine, jnp.mean(k, axis=0, keepdims=True), km)

    is_data = lane < HEAD_DIM

    for h in range(N_HEADS):
        ps = slice((h // 2) * LANES, (h // 2 + 1) * LANES)

        def head_base(t):
            p = t[:, ps]
            return p if h % 2 == 0 else pltpu.roll(p, HEAD_DIM, axis=1)

        qb, kb, vb = head_base(q), head_base(k), head_base(v)
        own_lane = HEAD_DIM + (h % HALF_HEADS) * AUX_BLOCKS + i
        q_out[h] = jnp.where(is_data, qb * QK_SCALE_LOG2, bias_by_half[h // HALF_HEADS]).astype(BF16)
        k_out[h] = jnp.where(is_data, kb, jnp.where(lane == own_lane, 1.0, 0.0)).astype(BF16)
        v_out[h] = jnp.where(is_data, vb, jnp.where(lane == HEAD_DIM, 1.0, 0.0)).T.astype(BF16)


def _rope_tables():
    half = ROT_DIM // 2
    invf = ROPE_THETA ** (-np.arange(half, dtype=np.float64) / half)
    lane = np.arange(LANES) % HEAD_DIM
    f = np.arange(half)[:, None]
    p_cos = ((lane[None, :] < ROT_DIM) & (lane[None, :] % half == f)).astype(np.float64)
    p_lo = -(lane[None, :] == f).astype(np.float64)
    p_hi = (lane[None, :] == f + half).astype(np.float64)
    z = np.zeros_like(p_cos)
    cos_rows = np.concatenate([p_cos, z, z], axis=1)
    sin_rows = np.concatenate([z, p_lo, p_hi], axis=1)
    expand = np.concatenate([cos_rows, sin_rows, cos_rows, sin_rows], axis=0)
    return (jnp.asarray(np.broadcast_to(invf[:, None], (half, MOBA_BLOCK)), F32), jnp.asarray(expand, F32))


def _moba_prep(zm, positions, qg, kg, hsum, batch, seq):
    TB = MOBA_BLOCK
    nb = seq // TB
    assert nb <= AUX_BLOCKS
    zm3 = zm.reshape(batch, seq, 3 * WIDTH)
    pos4 = positions.reshape(batch, nb, 1, TB)
    invf, expand = _rope_tables()
    aug = pl.BlockSpec((None, N_HEADS, TB, LANES), lambda b, i: (b, 0, i, 0))
    aug_shape = jax.ShapeDtypeStruct((batch, N_HEADS, seq, LANES), BF16)
    return pl.pallas_call(
        functools.partial(_moba_prep_kernel, n_blocks=nb),
        grid=(batch, nb),
        in_specs=[pl.BlockSpec((None, TB, 3 * WIDTH), lambda b, i: (b, i, 0)),
                  pl.BlockSpec((None, None, 1, TB), lambda b, i: (b, i, 0, 0)),
                  _const_spec(invf.shape), _const_spec(expand.shape), _const_spec(qg.shape),
                  _const_spec(kg.shape), _const_spec(hsum.shape)],
        out_specs=[aug, aug, pl.BlockSpec((None, N_HEADS, None, LANES, TB), lambda b, i: (b, 0, i, 0, 0))],
        out_shape=[aug_shape, aug_shape, jax.ShapeDtypeStruct((batch, N_HEADS, nb, LANES, TB), BF16)],
        scratch_shapes=[pltpu.VMEM((LANES, WIDTH), F32)],
        compiler_params=_params("parallel", "arbitrary"),
        name="moba_prep",
    )(zm3, pos4, invf, expand, qg, kg, hsum)


def _moba_attn_kernel(q_ref, k_ref, v_ref, o_ref, acc_ref, m_ref, s_ref, p_ref, a_ref):
    i = pl.program_id(2)
    TB = q_ref.shape[1]
    HP = q_ref.shape[0]
    n_steps = i + 1

    def block_of(tau):
        return jnp.where(tau == 0, i, jnp.minimum(tau - 1, jnp.maximum(i - 1, 0)))

    def qk(tau, par, masked=False):
        koff = pl.multiple_of(block_of(tau) * TB, TB)
        for hh in range(HP):
            s = _dot_nt(k_ref[hh, pl.ds(koff, TB), :], q_ref[hh])
            if masked:
                key = lax.broadcasted_iota(jnp.int32, (TB, TB), 0)
                qry = lax.broadcasted_iota(jnp.int32, (TB, TB), 1)
                s = jnp.where(key <= qry, s, -1e30)
            s_ref[par, hh] = s

    def softmax(par):
        for hh in range(HP):
            s = s_ref[par, hh]
            m_old = m_ref[hh]
            m_new = jnp.maximum(m_old, jnp.max(s, axis=0, keepdims=True))
            m_ref[hh] = m_new
            a_ref[par, hh] = jnp.exp2(m_old - m_new)
            p_ref[par, hh] = jnp.exp2(s - m_new).astype(BF16)

    def pv(tau, par):
        live = (tau < n_steps).astype(BF16)
        blk = block_of(tau)
        for hh in range(HP):
            rescale = jnp.where(tau < n_steps, a_ref[par, hh], 1.0)
            acc_ref[hh] = acc_ref[hh] * rescale + _dot(v_ref[hh, blk] * live, p_ref[par, hh])

    for hh in range(HP):
        m_ref[hh] = jnp.full((1, TB), -1e30, F32)
        acc_ref[hh] = jnp.zeros((LANES, TB), F32)
    qk(0, 0, masked=True)
    qk(1, 1)
    softmax(0)

    def slot_pair(pp, carry):
        t = 2 * pp
        qk(t + 2, 0)
        pv(t, 0)
        softmax(1)
        qk(t + 3, 1)
        pv(t + 1, 1)
        softmax(0)
        return carry

    lax.fori_loop(0, (n_steps + 1) // 2, slot_pair, 0)

    outs = []
    for hh in range(HP):
        acc = acc_ref[hh]
        outs.append((acc / acc[HEAD_DIM:HEAD_DIM + 1, :]).T)
    lane = lax.broadcasted_iota(jnp.int32, (TB, LANES), 1)
    for pr in range(HP // 2):
        o_ref[:, pr * LANES:(pr + 1) * LANES] = jnp.where(
            lane < HEAD_DIM, outs[2 * pr], pltpu.roll(outs[2 * pr + 1], HEAD_DIM, axis=1)).astype(o_ref.dtype)


ATTN_HEADS = 8


def _moba_attn(qa, ka, va, batch, seq):
    TB = MOBA_BLOCK
    nb = seq // TB
    HP = ATTN_HEADS
    return pl.pallas_call(
        _moba_attn_kernel,
        grid=(batch, N_HEADS // HP, nb),
        in_specs=[pl.BlockSpec((None, HP, TB, LANES), lambda b, p, i: (b, p, i, 0)),
                  pl.BlockSpec((None, HP, seq, LANES), lambda b, p, i: (b, p, 0, 0)),
                  pl.BlockSpec((None, HP, nb, LANES, TB), lambda b, p, i: (b, p, 0, 0, 0))],
        out_specs=pl.BlockSpec((None, TB, HP * HEAD_DIM), lambda b, p, i: (b, i, p)),
        out_shape=jax.ShapeDtypeStruct((batch, seq, WIDTH), BF16),
        scratch_shapes=[pltpu.VMEM((HP, LANES, TB), F32), pltpu.VMEM((HP, 1, TB), F32),
                        pltpu.VMEM((2, HP, TB, TB), F32), pltpu.VMEM((2, HP, TB, TB), BF16),
                        pltpu.VMEM((2, HP, 1, TB), F32)],
        compiler_params=_params("parallel", "parallel", "arbitrary"),
        name="moba_attn",
    )(qa, ka, va)


def _merge_kernel(x_ref, ya_ref, yb_ref, gate_ref, wa_ref, wb_ref, wo_ref, g2_ref, x1_ref, h2_ref):
    half = x_ref.shape[0] // 2
    halves = (slice(0, half), slice(half, 2 * half))
    ua = [_dot(ya_ref[rows, :], wa_ref[...]) for rows in halves]
    ub = [_dot(yb_ref[rows, :], wb_ref[...]) for rows in halves]
    for hi, rows in enumerate(halves):
        gate = gate_ref[rows, :].astype(F32)
        mix = (gate[:, 0:D_MODEL] * ua[hi] + gate[:, D_MODEL:] * ub[hi]).astype(BF16)
        x1 = x_ref[rows, :] + _dot(mix, wo_ref[...])
        x1_ref[rows, :] = x1
        ms = jnp.mean(x1 * x1, axis=-1, keepdims=True)
        h2_ref[rows, :] = (x1 * lax.rsqrt(ms + NORM_EPS) * g2_ref[...]).astype(BF16)


def _merge(x2, ya, yb, gates, wa, wb, wo, g2, tm):
    t = x2.shape[0]
    row = lambda n: pl.BlockSpec((tm, n), lambda i: (i, 0))
    return pl.pallas_call(
        _merge_kernel,
        grid=(t // tm,),
        in_specs=[row(D_MODEL), row(WIDTH), row(WIDTH), row(2 * D_MODEL), _const_spec(wa.shape),
                  _const_spec(wb.shape), _const_spec(wo.shape), _const_spec(g2.shape)],
        out_specs=[row(D_MODEL), row(D_MODEL)],
        out_shape=[jax.ShapeDtypeStruct((t, D_MODEL), F32), jax.ShapeDtypeStruct((t, D_MODEL), BF16)],
        compiler_params=_params("parallel"),
        name="merge",
    )(x2, ya, yb, gates, wa, wb, wo, g2)


FFN_HALO = 16
FFN_COLS = (768, 768, 768, 512)


def _ffn_kernel(h_ref, hp_ref, x1_ref, wua_ref, wub_ref, cw_ref, cb_ref, wd_ref, o_ref, *, tiles_per_seq):
    i = pl.program_id(0)
    tm = h_ref.shape[0]
    h = h_ref[...]
    halo = jnp.where(i % tiles_per_seq == 0, jnp.zeros_like(hp_ref[...]), hp_ref[...])
    h_ext = jnp.concatenate([halo, h], axis=0)
    starts = np.cumsum((0,) + FFN_COLS)
    groups = [slice(int(starts[g]), int(starts[g + 1])) for g in range(len(FFN_COLS))]

    def up(cs):
        return _dot(h_ext, wua_ref[:, cs]), _dot(h, wub_ref[:, cs])

    acc = x1_ref[...]
    nxt = up(groups[0])
    for g, cs in enumerate(groups):
        a, b = nxt
        if g + 1 < len(groups):
            nxt = up(groups[g + 1])
        conv = (a[FFN_HALO:, :] * cw_ref[2:3, cs] + a[FFN_HALO - 1:FFN_HALO - 1 + tm, :] * cw_ref[1:2, cs]
                + a[FFN_HALO - 2:FFN_HALO - 2 + tm, :] * cw_ref[0:1, cs] + cb_ref[:, cs])
        gelu = 0.5 * conv * (1.0 + lax.erf(conv * (2.0 ** -0.5)))
        acc = acc + _dot((gelu * b).astype(BF16), wd_ref[cs, :])
    o_ref[...] = acc


def _ffn(h2, x1, wua, wub, cw, cb, wd, tm, seq):
    t = h2.shape[0]
    row = lambda n: pl.BlockSpec((tm, n), lambda i: (i, 0))
    halo = pl.BlockSpec((FFN_HALO, D_MODEL), lambda i: (jnp.maximum(i * (tm // FFN_HALO) - 1, 0), 0))
    return pl.pallas_call(
        functools.partial(_ffn_kernel, tiles_per_seq=seq // tm),
        grid=(t // tm,),
        in_specs=[row(D_MODEL), halo, row(D_MODEL), _const_spec(wua.shape), _const_spec(wub.shape),
                  _const_spec(cw.shape), _const_spec(cb.shape), _const_spec(wd.shape)],
        out_specs=row(D_MODEL),
        out_shape=jax.ShapeDtypeStruct((t, D_MODEL), F32),
        compiler_params=_params("parallel"),
        name="ffn",
    )(h2, h2, x1, wua, wub, cw, cb, wd)


def _pad_lora_cols(a):
    z = lambda n: jnp.zeros(a.shape[:-1] + (n,), a.dtype)
    o1, o2 = DECAY_LORA, DECAY_LORA + AAA_LORA
    return jnp.concatenate([a[..., :o1], z(LANES - DECAY_LORA), a[..., o1:o2], z(LANES - AAA_LORA),
                            a[..., o2:], z(2 * LANES - GATE_LORA)], axis=-1)


def _pad_rows(a, n):
    return jnp.concatenate([a, jnp.zeros((n - a.shape[0],) + a.shape[1:], a.dtype)], axis=0)


def kernel(x, positions, norm1_g, w_in, rwkv_mu, w_decay_up, decay_bias, w_aaa_up, aaa_bias, w_gate_up, rwkv_k_k, rwkv_k_a, rwkv_r_k, rwkv_ln_g, rwkv_ln_b, q_norm_g, k_norm_g, w_branch_a, w_branch_b, w_out, norm2_g, w_ffn_up, ffn_conv_w, ffn_conv_b, w_ffn_down):
    batch, seq, _ = x.shape
    depth = norm1_g.shape[0]
    assert seq % MOBA_BLOCK == 0 and seq % 512 == 0
    t = batch * seq
    row = lambda a: a.reshape(1, -1)
    c3 = 3 * WIDTH
    rwkv_in = c3 + DECAY_LORA + AAA_LORA + GATE_LORA
    hsum = _rwkv_masks()[1]

    x2 = x.reshape(t, D_MODEL)
    for l in range(depth):
        wi = w_in[l]
        wr = wi[:, :c3].astype(BF16)
        wl = _pad_lora_cols(wi[:, c3:rwkv_in]).astype(BF16)
        wm = wi[:, rwkv_in:rwkv_in + c3].astype(BF16)
        wg = wi[:, rwkv_in + c3:].astype(BF16)
        zr, zl, zm, gates = _inproj(x2, row(norm1_g[l]), wr, wl, wm, wg, tm=512)

        consts = [row(rwkv_mu[l][:c3]), _pad_lora_cols(row(rwkv_mu[l][c3:])),
                  _pad_rows(w_decay_up[l], LANES).astype(BF16), row(decay_bias[l]),
                  _pad_rows(w_aaa_up[l], LANES).astype(BF16), row(aaa_bias[l]),
                  _pad_rows(w_gate_up[l], 2 * LANES).astype(BF16),
                  row(rwkv_k_k[l]), row(rwkv_k_a[l]), row(rwkv_r_k[l]),
                  row(rwkv_ln_g[l]), row(rwkv_ln_b[l])]
        ya = _rwkv(zr, zl, consts, batch, seq).reshape(t, WIDTH)

        tile8 = lambda a: row(jnp.tile(a, N_HEADS))
        qa, ka, va = _moba_prep(zm, positions, tile8(q_norm_g[l]), tile8(k_norm_g[l]), hsum, batch, seq)
        yb = _moba_attn(qa, ka, va, batch, seq).reshape(t, WIDTH)

        x1, h2 = _merge(x2, ya, yb, gates, w_branch_a[l].astype(BF16), w_branch_b[l].astype(BF16),
                        w_out[l].astype(BF16), row(norm2_g[l]), tm=512)

        wu = w_ffn_up[l]
        x2 = _ffn(h2, x1, wu[:, :D_FF].astype(BF16), wu[:, D_FF:].astype(BF16), ffn_conv_w[l],
                  row(ffn_conv_b[l]), w_ffn_down[l].astype(BF16), tm=512, seq=seq)
    return x2.reshape(batch, seq, D_MODEL)
```

```python
import functools

import numpy as np
import jax
import jax.numpy as jnp
from jax import lax
from jax.experimental import pallas as pl
from jax.experimental.pallas import tpu as pltpu

F32 = jnp.float32
BF16 = jnp.bfloat16
HI = lax.Precision.HIGHEST

D_MODEL = 1024
HEAD_DIM = 64
N_HEADS = 8
WIDTH = N_HEADS * HEAD_DIM
DECAY_LORA = 64
AAA_LORA = 64
GATE_LORA = 160
LORA_PAD = 512
MOBA_BLOCK = 256
MOBA_TOPK = 3
ROT_DIM = HEAD_DIM // 4
ROPE_THETA = 500000.0
D_FF = 2816
NORM_EPS = 1e-6
GN_EPS = 64e-5
LANES = 128
CHUNK = 64
NEG_BIG = -32768.0
VMEM_LIMIT = 56 * 1024 * 1024


def _sigmoid(x):
    return 1.0 / (1.0 + jnp.exp(-x))


def _dot(a, b, precision=None):
    return jnp.dot(a, b, preferred_element_type=F32, precision=precision)


def _dot_nt(a, b, precision=None):
    return lax.dot_general(a, b, (((1,), (1,)), ((), ())), preferred_element_type=F32,
                           precision=precision)


def _head_sums(x, bd_ones):
    g = bd_ones.shape[0]
    return jnp.concatenate([_dot(x[:, j:j + g].astype(BF16), bd_ones) for j in range(0, x.shape[1], g)],
                           axis=1)


def _dot_tn(a, b, precision=None):
    return lax.dot_general(a, b, (((0,), (0,)), ((), ())), preferred_element_type=F32,
                           precision=precision)


def _params(*sem):
    return pltpu.CompilerParams(dimension_semantics=sem, vmem_limit_bytes=VMEM_LIMIT)


def _const_spec(shape):
    nd = len(shape)
    return pl.BlockSpec(shape, lambda *_: (0,) * nd, pipeline_mode=pl.Buffered(1))


def _inproj_kernel(x_ref, g_ref, wr_ref, wl_ref, wm_ref, wg_ref,
                   zr_ref, zl_ref, zm_ref, gate_ref):
    half = x_ref.shape[0] // 2
    for rows in (slice(0, half), slice(half, 2 * half)):
        x = x_ref[rows, :]
        ms = jnp.mean(x * x, axis=-1, keepdims=True)
        h = (x * lax.rsqrt(ms + NORM_EPS) * g_ref[...]).astype(BF16)
        zr_ref[rows, :] = _dot(h, wr_ref[...])
        zl_ref[rows, :] = _dot(h, wl_ref[...])
        zm_ref[rows, :] = _dot(h, wm_ref[...])
        gate_ref[rows, :] = _sigmoid(_dot(h, wg_ref[...])).astype(BF16)


def _inproj(x2, g, wr, wl, wm, wg, tm):
    t = x2.shape[0]
    row = lambda n: pl.BlockSpec((tm, n), lambda i: (i, 0))
    return pl.pallas_call(
        _inproj_kernel,
        grid=(t // tm,),
        in_specs=[row(D_MODEL), _const_spec(g.shape), _const_spec(wr.shape), _const_spec(wl.shape),
                  _const_spec(wm.shape), _const_spec(wg.shape)],
        out_specs=[row(3 * WIDTH), row(LORA_PAD), row(3 * WIDTH), row(2 * D_MODEL)],
        out_shape=[jax.ShapeDtypeStruct((t, 3 * WIDTH), F32),
                   jax.ShapeDtypeStruct((t, LORA_PAD), F32),
                   jax.ShapeDtypeStruct((t, 3 * WIDTH), F32),
                   jax.ShapeDtypeStruct((t, 2 * D_MODEL), BF16)],
        compiler_params=_params("parallel"),
        name="inproj",
    )(x2, g, wr, wl, wm, wg)


RW_ROWS = 256
RW_SEQS = 4
RW_GROUP = 256


def _rwkv_kernel(zr_ref, zrp_ref, zl_ref, zlp_ref, mur_ref, mul_ref, wd_ref, db_ref, wa_ref,
                 ab_ref, wg_ref, kk_ref, ka_ref, rk_ref, lng_ref, lnb_ref, cum_ref, bd_ref,
                 o_ref, state_ref):
    c = pl.program_id(1)
    n_seq, TS = zr_ref.shape[0], zr_ref.shape[1]
    TL = n_seq * TS
    L = CHUNK
    G = RW_GROUP

    @pl.when(c == 0)
    def _():
        state_ref[...] = jnp.zeros_like(state_ref)

    def token_shift(z_ref, zp_ref, mu_ref):
        z = z_ref[...].reshape(TL, z_ref.shape[2])
        row = lax.broadcasted_iota(jnp.int32, z.shape, 0)
        zs = pltpu.roll(z, 1, axis=0)
        for s in range(n_seq):
            zs = jnp.where(row == s * TS, jnp.where(c == 0, 0.0, zp_ref[s, 7:8, :]), zs)
        return z + mu_ref[...] * (zs - z)

    zr = token_shift(zr_ref, zrp_ref, mur_ref)
    zl = token_shift(zl_ref, zlp_ref, mul_ref)
    r = zr[:, 0:WIDTH]
    k = zr[:, WIDTH:2 * WIDTH]
    v = zr[:, 2 * WIDTH:3 * WIDTH]
    xw = zl[:, 0:LANES]
    xa = zl[:, LANES:2 * LANES]
    xg = zl[:, 2 * LANES:4 * LANES]

    dd = db_ref[...] + _dot(jnp.tanh(xw).astype(BF16), wd_ref[...])
    w_log = -(jnp.maximum(-dd, 0.0) + jnp.log(1.0 + jnp.exp(-jnp.abs(dd)))) - 0.5
    lw = -jnp.exp(w_log)
    asig = _sigmoid(ab_ref[...] + _dot(xa.astype(BF16), wa_ref[...]))
    g = _dot(_sigmoid(xg).astype(BF16), wg_ref[...])
    bdm = bd_ref[...]
    kkf = k * kk_ref[...]
    kk = kkf * lax.rsqrt(jnp.maximum(_head_sums(kkf * kkf, bdm), 1e-24))
    kmod = k * (1.0 + (asig - 1.0) * ka_ref[...])
    bonus = _head_sums(r * kmod * rk_ref[...], bdm) * v
    b = kk * asig

    lw_hi = lw.astype(BF16)
    lw_lo = (lw - lw_hi.astype(F32)).astype(BF16)
    cum = cum_ref[...]
    cums = [_dot(cum, lw_hi[ci * L:(ci + 1) * L]) + _dot(cum, lw_lo[ci * L:(ci + 1) * L]) for ci in range(TL // L)]
    cw = jnp.concatenate([cs[0:L] for cs in cums], axis=0)
    cw_end = jnp.concatenate([cs[L:2 * L] for cs in cums], axis=0)
    e_neg = jnp.exp(-cw)
    e_end = jnp.exp(cw_end - cw)
    a_til = (-kk * jnp.exp(cw - lw)).astype(BF16)
    r_til = (r * jnp.exp(cw)).astype(BF16)
    b_til = (b * e_neg).astype(BF16)
    k_til = (kmod * e_neg).astype(BF16)
    b_hat = b * e_end
    k_hat = kmod * e_end
    w_end = jnp.exp(cw_end)
    vb = v.astype(BF16)

    def bd(x):
        return jnp.concatenate([x.astype(BF16)] * (G // L), axis=0) * bdm

    lane_lo = lax.broadcasted_iota(jnp.int32, (L, LANES), 1) < HEAD_DIM

    def head_t(x):
        t = jnp.concatenate([x, x], axis=0).T
        return jnp.concatenate([jnp.where(lane_lo, t[2 * j * L:(2 * j + 1) * L], t[(2 * j + 1) * L:(2 * j + 2) * L])
                                for j in range(G // LANES)], axis=1)

    row2 = lax.broadcasted_iota(jnp.int32, (2 * L, G), 0)
    lane_t = lax.broadcasted_iota(jnp.int32, (2 * L, G), 1) % L
    tri2 = lane_t < jnp.where(row2 < L, row2, row2 - L + 1)
    eye = (lax.broadcasted_iota(jnp.int32, (L, G), 1) % L
           == lax.broadcasted_iota(jnp.int32, (L, G), 0)).astype(F32)

    n_chunks, n_groups = TL // L, WIDTH // G
    pairs = [(ci, gi) for ci in range(n_chunks) for gi in range(n_groups)]

    def blk(x, cg):
        ci, gi = cg
        return x[ci * L:(ci + 1) * L, gi * G:(gi + 1) * G]

    p = {cg: jnp.concatenate([blk(a_til, cg), blk(r_til, cg)], axis=0) for cg in pairs}
    ab_rb = {cg: jnp.where(tri2, _dot_nt(p[cg], bd(blk(b_til, cg))), 0.0) for cg in pairs}
    ak_rk = {cg: jnp.where(tri2, _dot_nt(p[cg], bd(blk(k_til, cg))), 0.0) for cg in pairs}
    tinv = {cg: eye + ab_rb[cg][0:L] for cg in pairs}
    pw = {cg: _dot(ab_rb[cg][0:L].astype(BF16), bd(ab_rb[cg][0:L])) for cg in pairs}
    for _ in range(int(np.log2(L)) - 2):
        both = {cg: _dot(jnp.concatenate([pw[cg], tinv[cg]], axis=0).astype(BF16), bd(pw[cg])) for cg in pairs}
        pw = {cg: both[cg][0:L] for cg in pairs}
        tinv = {cg: tinv[cg] + both[cg][L:2 * L] for cg in pairs}
    tinv = {cg: (tinv[cg] + _dot(tinv[cg].astype(BF16), bd(pw[cg]))).astype(BF16) for cg in pairs}
    bd_v = {cg: bd(blk(vb, cg)) for cg in pairs}
    kv = {cg: _dot(ak_rk[cg].astype(BF16), bd_v[cg]) for cg in pairs}
    kt_v = {cg: _dot(head_t(blk(k_hat, cg)).astype(BF16), bd_v[cg]) for cg in pairs}
    arb_bt = {cg: jnp.concatenate([ab_rb[cg][L:2 * L], head_t(blk(b_hat, cg))], axis=0).astype(BF16)
              for cg in pairs}
    w_row = {cg: head_t(blk(w_end, cg)) for cg in pairs}

    chains = [(s, gi) for s in range(n_seq) for gi in range(n_groups)]
    seq_chunks = TS // L
    st = {sg: state_ref[sg[0] * n_groups + sg[1]] for sg in chains}
    y_blk = {}
    for ci in range(seq_chunks):
        cg = {sg: (sg[0] * seq_chunks + ci, sg[1]) for sg in chains}
        ps = {sg: _dot(p[cg[sg]], bd(st[sg])) for sg in chains}
        u = {sg: _dot(tinv[cg[sg]], bd(ps[sg][0:L] + kv[cg[sg]][0:L])) for sg in chains}
        yu = {sg: _dot(arb_bt[cg[sg]], bd(u[sg])) for sg in chains}
        st = {sg: w_row[cg[sg]] * st[sg] + yu[sg][L:2 * L] + kt_v[cg[sg]] for sg in chains}
        for sg in chains:
            y_blk[cg[sg]] = ps[sg][L:2 * L] + kv[cg[sg]][L:2 * L] + yu[sg][0:L]
    for sg in chains:
        state_ref[sg[0] * n_groups + sg[1]] = st[sg]
    y = jnp.concatenate([jnp.concatenate([y_blk[(ci, gi)] for gi in range(n_groups)], axis=1)
                         for ci in range(n_chunks)], axis=0)

    inv_n = 1.0 / HEAD_DIM
    mu = _head_sums(y, bdm) * inv_n
    yc = y - mu
    var = _head_sums(yc * yc, bdm) * inv_n
    yn = yc * lax.rsqrt(var + GN_EPS) * lng_ref[...] + lnb_ref[...]
    o_ref[...] = ((yn + bonus) * g).astype(o_ref.dtype).reshape(o_ref.shape)


def _rwkv_masks():
    t = np.arange(CHUNK)
    tri = t[:, None] >= t[None, :]
    cum = np.concatenate([tri, np.ones_like(tri)], axis=0)
    g = np.arange(RW_GROUP)
    bd = (g[:, None] // HEAD_DIM) == (g[None, :] // HEAD_DIM)
    return [jnp.asarray(m, BF16) for m in (cum, bd)]


def _rwkv(zr, zl, consts, batch, seq):
    TL = RW_ROWS
    nt = seq // TL
    ns = RW_SEQS if batch % RW_SEQS == 0 else 1
    cur = lambda n: pl.BlockSpec((ns, TL, n), lambda b, c: (b, c, 0))
    prv = lambda n: pl.BlockSpec((ns, 8, n), lambda b, c: (b, jnp.maximum(c * (TL // 8) - 1, 0), 0))
    zr3 = zr.reshape(batch, seq, 3 * WIDTH)
    zl3 = zl.reshape(batch, seq, LORA_PAD)
    consts = list(consts) + _rwkv_masks()
    return pl.pallas_call(
        _rwkv_kernel,
        grid=(batch // ns, nt),
        in_specs=[cur(3 * WIDTH), prv(3 * WIDTH), cur(LORA_PAD), prv(LORA_PAD)]
                 + [_const_spec(a.shape) for a in consts],
        out_specs=pl.BlockSpec((ns, TL, WIDTH), lambda b, c: (b, c, 0)),
        out_shape=jax.ShapeDtypeStruct((batch, seq, WIDTH), BF16),
        scratch_shapes=[pltpu.VMEM((ns * (WIDTH // RW_GROUP), HEAD_DIM, RW_GROUP), F32)],
        compiler_params=_params("parallel", "arbitrary"),
        name="rwkv",
    )(zr3, zr3, zl3, zl3, *consts)


QK_SCALE_LOG2 = HEAD_DIM ** -0.5 * float(np.log2(np.e))
AUX_BLOCKS = LANES // N_HEADS
HALF_HEADS = N_HEADS // 2
VT_ROWS = 80


def _moba_prep_kernel(zm_ref, pos_ref, invf_ref, ropep_ref, qg_ref, kg_ref, hsum_ref,
                      q_out, k_out, v_out, km_ref, *, n_blocks):
    i = pl.program_id(1)
    TB = zm_ref.shape[0]

    @pl.when(i == 0)
    def _():
        km_ref[...] = jnp.zeros_like(km_ref)

    zm = zm_ref[...]
    ang = invf_ref[...] * pos_ref[...].astype(F32)
    cs = jnp.concatenate([jnp.cos(ang), jnp.sin(ang)], axis=0)
    cs_hi = cs.astype(BF16).astype(F32)
    tab = _dot_tn(jnp.concatenate([cs_hi, cs - cs_hi], axis=0), ropep_ref[...])
    lane = lax.broadcasted_iota(jnp.int32, (TB, LANES), 1)
    rep = WIDTH // LANES
    half = ROT_DIM // 2
    cos = jnp.concatenate([tab[:, 0:LANES] + jnp.where(lane % HEAD_DIM >= ROT_DIM, 1.0, 0.0)] * rep, axis=-1)
    sin_lo = jnp.concatenate([tab[:, LANES:2 * LANES]] * rep, axis=-1)
    sin_hi = jnp.concatenate([tab[:, 2 * LANES:3 * LANES]] * rep, axis=-1)
    hsum = hsum_ref[...]

    def norm_rope(t, gain):
        ms = _head_sums(t * t, hsum) * (1.0 / HEAD_DIM)
        t = t * lax.rsqrt(ms + NORM_EPS) * gain
        return (t * cos + pltpu.roll(t, WIDTH - half, axis=1) * sin_lo
                + pltpu.roll(t, half, axis=1) * sin_hi)

    q = norm_rope(zm[:, 0:WIDTH], qg_ref[...])
    k = norm_rope(zm[:, WIDTH:2 * WIDTH], kg_ref[...])
    v = zm[:, 2 * WIDTH:3 * WIDTH]

    km = km_ref[...]
    gate = _dot_nt(km, q, HI).reshape(N_HEADS, AUX_BLOCKS, TB)
    n_idx = lax.broadcasted_iota(jnp.int32, gate.shape, 1)
    gsel = jnp.where(n_idx < i, gate, -jnp.inf)
    picked = n_idx < 0
    for _ in range(MOBA_TOPK):
        m = jnp.max(gsel, axis=1, keepdims=True)
        idx = jnp.min(jnp.where(gsel == m, n_idx, AUX_BLOCKS), axis=1, keepdims=True)
        pick = n_idx == idx
        picked = picked | (pick & (m > -jnp.inf))
        gsel = jnp.where(pick, -jnp.inf, gsel)
    keep = picked | (n_idx == i) | (n_idx >= n_blocks)
    bias = jnp.where(keep, 0.0, NEG_BIG).reshape(LANES, TB).T
    bias_by_half = [pltpu.roll(bias, HEAD_DIM, axis=1), bias]

    rowk = lax.broadcasted_iota(jnp.int32, km.shape, 0)
    lanek = lax.broadcasted_iota(jnp.int32, km.shape, 1)
    mine = (rowk % AUX_BLOCKS == i) & (lanek // HEAD_DIM == rowk // AUX_BLOCKS)
    km_ref[...] = jnp.where(mine, jnp.mean(k, axis=0, keepdims=True), km)

    is_data = lane < HEAD_DIM

    for h in range(N_HEADS):
        ps = slice((h // 2) * LANES, (h // 2 + 1) * LANES)

        def head_base(t):
            p = t[:, ps]
            return p if h % 2 == 0 else pltpu.roll(p, HEAD_DIM, axis=1)

        qb, kb, vb = head_base(q), head_base(k), head_base(v)
        own_lane = HEAD_DIM + (h % HALF_HEADS) * AUX_BLOCKS + i
        q_out[h] = jnp.where(is_data, qb * QK_SCALE_LOG2, bias_by_half[h // HALF_HEADS]).astype(BF16)
        k_out[h] = jnp.where(is_data, kb, jnp.where(lane == own_lane, 1.0, 0.0)).astype(BF16)
        v_out[h] = jnp.where(is_data, vb, jnp.where(lane == HEAD_DIM, 1.0, 0.0)).T[0:VT_ROWS].astype(BF16)


def _rope_tables():
    half = ROT_DIM // 2
    invf = ROPE_THETA ** (-np.arange(half, dtype=np.float64) / half)
    lane = np.arange(LANES) % HEAD_DIM
    f = np.arange(half)[:, None]
    p_cos = ((lane[None, :] < ROT_DIM) & (lane[None, :] % half == f)).astype(np.float64)
    p_lo = -(lane[None, :] == f).astype(np.float64)
    p_hi = (lane[None, :] == f + half).astype(np.float64)
    z = np.zeros_like(p_cos)
    cos_rows = np.concatenate([p_cos, z, z], axis=1)
    sin_rows = np.concatenate([z, p_lo, p_hi], axis=1)
    expand = np.concatenate([cos_rows, sin_rows, cos_rows, sin_rows], axis=0)
    return (jnp.asarray(np.broadcast_to(invf[:, None], (half, MOBA_BLOCK)), F32), jnp.asarray(expand, F32))


def _moba_prep(zm, positions, qg, kg, hsum, batch, seq):
    TB = MOBA_BLOCK
    nb = seq // TB
    assert nb <= AUX_BLOCKS
    zm3 = zm.reshape(batch, seq, 3 * WIDTH)
    pos4 = positions.reshape(batch, nb, 1, TB)
    invf, expand = _rope_tables()
    aug = pl.BlockSpec((None, N_HEADS, TB, LANES), lambda b, i: (b, 0, i, 0))
    aug_shape = jax.ShapeDtypeStruct((batch, N_HEADS, seq, LANES), BF16)
    return pl.pallas_call(
        functools.partial(_moba_prep_kernel, n_blocks=nb),
        grid=(batch, nb),
        in_specs=[pl.BlockSpec((None, TB, 3 * WIDTH), lambda b, i: (b, i, 0)),
                  pl.BlockSpec((None, None, 1, TB), lambda b, i: (b, i, 0, 0)),
                  _const_spec(invf.shape), _const_spec(expand.shape), _const_spec(qg.shape),
                  _const_spec(kg.shape), _const_spec(hsum.shape)],
        out_specs=[aug, aug, pl.BlockSpec((None, N_HEADS, None, VT_ROWS, TB), lambda b, i: (b, 0, i, 0, 0))],
        out_shape=[aug_shape, aug_shape, jax.ShapeDtypeStruct((batch, N_HEADS, nb, VT_ROWS, TB), BF16)],
        scratch_shapes=[pltpu.VMEM((LANES, WIDTH), F32)],
        compiler_params=_params("parallel", "arbitrary"),
        name="moba_prep",
    )(zm3, pos4, invf, expand, qg, kg, hsum)


def _moba_attn_kernel(q_ref, k_ref, v_ref, o_ref, acc_ref, m_ref, s_ref, p_ref, a_ref, mx_ref):
    i = pl.program_id(2)
    TB = q_ref.shape[1]
    HP = q_ref.shape[0]
    n_steps = i + 1

    def block_of(tau):
        return jnp.where(tau == 0, i, jnp.minimum(tau - 1, jnp.maximum(i - 1, 0)))

    def qk(tau, par, masked=False):
        koff = pl.multiple_of(block_of(tau) * TB, TB)
        for hh in range(HP):
            s = _dot_nt(k_ref[hh, pl.ds(koff, TB), :], q_ref[hh])
            if masked:
                key = lax.broadcasted_iota(jnp.int32, (TB, TB), 0)
                qry = lax.broadcasted_iota(jnp.int32, (TB, TB), 1)
                s = jnp.where(key <= qry, s, -1e30)
            s_ref[par, hh] = s
            mx_ref[par, hh] = jnp.max(s, axis=0, keepdims=True)

    def softmax(par):
        for hh in range(HP):
            m_old = m_ref[hh]
            m_new = jnp.maximum(m_old, mx_ref[par, hh])
            m_ref[hh] = m_new
            a_ref[par, hh] = jnp.exp2(m_old - m_new)
            p_ref[par, hh] = jnp.exp2(s_ref[par, hh] - m_new).astype(BF16)

    def pv(tau, par):
        live = (tau < n_steps).astype(BF16)
        blk = block_of(tau)
        for hh in range(HP):
            rescale = jnp.where(tau < n_steps, a_ref[par, hh], 1.0)
            acc_ref[hh] = acc_ref[hh] * rescale + _dot(v_ref[hh, blk] * live, p_ref[par, hh])

    for hh in range(HP):
        m_ref[hh] = jnp.full((1, TB), -1e30, F32)
        acc_ref[hh] = jnp.zeros((VT_ROWS, TB), F32)
    qk(0, 0, masked=True)
    qk(1, 1)
    softmax(0)

    def slot_pair(pp, carry):
        t = 2 * pp
        qk(t + 2, 0)
        pv(t, 0)
        softmax(1)
        qk(t + 3, 1)
        pv(t + 1, 1)
        softmax(0)
        return carry

    lax.fori_loop(0, (n_steps + 1) // 2, slot_pair, 0)

    outs = []
    for hh in range(HP):
        acc = acc_ref[hh]
        out_t = jnp.concatenate([acc / acc[HEAD_DIM:HEAD_DIM + 1, :], jnp.zeros((LANES - VT_ROWS, TB), F32)],
                                axis=0)
        outs.append(out_t.T)
    lane = lax.broadcasted_iota(jnp.int32, (TB, LANES), 1)
    for pr in range(HP // 2):
        o_ref[:, pr * LANES:(pr + 1) * LANES] = jnp.where(
            lane < HEAD_DIM, outs[2 * pr], pltpu.roll(outs[2 * pr + 1], HEAD_DIM, axis=1)).astype(o_ref.dtype)


ATTN_HEADS = 8


def _moba_attn(qa, ka, va, batch, seq):
    TB = MOBA_BLOCK
    nb = seq // TB
    HP = ATTN_HEADS
    return pl.pallas_call(
        _moba_attn_kernel,
        grid=(batch, N_HEADS // HP, nb),
        in_specs=[pl.BlockSpec((None, HP, TB, LANES), lambda b, p, i: (b, p, i, 0)),
                  pl.BlockSpec((None, HP, seq, LANES), lambda b, p, i: (b, p, 0, 0)),
                  pl.BlockSpec((None, HP, nb, VT_ROWS, TB), lambda b, p, i: (b, p, 0, 0, 0))],
        out_specs=pl.BlockSpec((None, TB, HP * HEAD_DIM), lambda b, p, i: (b, i, p)),
        out_shape=jax.ShapeDtypeStruct((batch, seq, WIDTH), BF16),
        scratch_shapes=[pltpu.VMEM((HP, VT_ROWS, TB), F32), pltpu.VMEM((HP, 1, TB), F32),
                        pltpu.VMEM((2, HP, TB, TB), F32), pltpu.VMEM((2, HP, TB, TB), BF16),
                        pltpu.VMEM((2, HP, 1, TB), F32), pltpu.VMEM((2, HP, 1, TB), F32)],
        compiler_params=_params("parallel", "parallel", "arbitrary"),
        name="moba_attn",
    )(qa, ka, va)


def _merge_kernel(x_ref, ya_ref, yb_ref, gate_ref, wa_ref, wb_ref, wo_ref, g2_ref, x1_ref, h2_ref):
    half = x_ref.shape[0] // 2
    halves = (slice(0, half), slice(half, 2 * half))
    ua = [_dot(ya_ref[rows, :], wa_ref[...]) for rows in halves]
    ub = [_dot(yb_ref[rows, :], wb_ref[...]) for rows in halves]
    for hi, rows in enumerate(halves):
        gate = gate_ref[rows, :].astype(F32)
        mix = (gate[:, 0:D_MODEL] * ua[hi] + gate[:, D_MODEL:] * ub[hi]).astype(BF16)
        x1 = x_ref[rows, :] + _dot(mix, wo_ref[...])
        x1_ref[rows, :] = x1
        ms = jnp.mean(x1 * x1, axis=-1, keepdims=True)
        h2_ref[rows, :] = (x1 * lax.rsqrt(ms + NORM_EPS) * g2_ref[...]).astype(BF16)


def _merge(x2, ya, yb, gates, wa, wb, wo, g2, tm):
    t = x2.shape[0]
    row = lambda n: pl.BlockSpec((tm, n), lambda i: (i, 0))
    return pl.pallas_call(
        _merge_kernel,
        grid=(t // tm,),
        in_specs=[row(D_MODEL), row(WIDTH), row(WIDTH), row(2 * D_MODEL), _const_spec(wa.shape),
                  _const_spec(wb.shape), _const_spec(wo.shape), _const_spec(g2.shape)],
        out_specs=[row(D_MODEL), row(D_MODEL)],
        out_shape=[jax.ShapeDtypeStruct((t, D_MODEL), F32), jax.ShapeDtypeStruct((t, D_MODEL), BF16)],
        compiler_params=_params("parallel"),
        name="merge",
    )(x2, ya, yb, gates, wa, wb, wo, g2)


FFN_HALO = 16
FFN_COLS = (768, 768, 768, 512)


def _ffn_kernel(h_ref, hp_ref, x1_ref, wua_ref, wub_ref, cw_ref, cb_ref, wd_ref, o_ref, *, tiles_per_seq):
    i = pl.program_id(0)
    tm = h_ref.shape[0]
    h = h_ref[...]
    halo = jnp.where(i % tiles_per_seq == 0, jnp.zeros_like(hp_ref[...]), hp_ref[...])
    h_ext = jnp.concatenate([halo, h], axis=0)
    starts = np.cumsum((0,) + FFN_COLS)
    groups = [slice(int(starts[g]), int(starts[g + 1])) for g in range(len(FFN_COLS))]

    def up(cs):
        return _dot(h_ext, wua_ref[:, cs]), _dot(h, wub_ref[:, cs])

    acc = x1_ref[...]
    nxt = up(groups[0])
    for g, cs in enumerate(groups):
        a, b = nxt
        if g + 1 < len(groups):
            nxt = up(groups[g + 1])
        conv = (a[FFN_HALO:, :] * cw_ref[2:3, cs] + a[FFN_HALO - 1:FFN_HALO - 1 + tm, :] * cw_ref[1:2, cs]
                + a[FFN_HALO - 2:FFN_HALO - 2 + tm, :] * cw_ref[0:1, cs] + cb_ref[:, cs])
        gelu = 0.5 * conv * (1.0 + lax.erf(conv * (2.0 ** -0.5)))
        acc = acc + _dot((gelu * b).astype(BF16), wd_ref[cs, :])
    o_ref[...] = acc


def _ffn(h2, x1, wua, wub, cw, cb, wd, tm, seq):
    t = h2.shape[0]
    row = lambda n: pl.BlockSpec((tm, n), lambda i: (i, 0))
    halo = pl.BlockSpec((FFN_HALO, D_MODEL), lambda i: (jnp.maximum(i * (tm // FFN_HALO) - 1, 0), 0))
    return pl.pallas_call(
        functools.partial(_ffn_kernel, tiles_per_seq=seq // tm),
        grid=(t // tm,),
        in_specs=[row(D_MODEL), halo, row(D_MODEL), _const_spec(wua.shape), _const_spec(wub.shape),
                  _const_spec(cw.shape), _const_spec(cb.shape), _const_spec(wd.shape)],
        out_specs=row(D_MODEL),
        out_shape=jax.ShapeDtypeStruct((t, D_MODEL), F32),
        compiler_params=_params("parallel"),
        name="ffn",
    )(h2, h2, x1, wua, wub, cw, cb, wd)


def _pad_lora_cols(a):
    z = lambda n: jnp.zeros(a.shape[:-1] + (n,), a.dtype)
    o1, o2 = DECAY_LORA, DECAY_LORA + AAA_LORA
    return jnp.concatenate([a[..., :o1], z(LANES - DECAY_LORA), a[..., o1:o2], z(LANES - AAA_LORA),
                            a[..., o2:], z(2 * LANES - GATE_LORA)], axis=-1)


def _pad_rows(a, n):
    return jnp.concatenate([a, jnp.zeros((n - a.shape[0],) + a.shape[1:], a.dtype)], axis=0)


def kernel(x, positions, norm1_g, w_in, rwkv_mu, w_decay_up, decay_bias, w_aaa_up, aaa_bias, w_gate_up, rwkv_k_k, rwkv_k_a, rwkv_r_k, rwkv_ln_g, rwkv_ln_b, q_norm_g, k_norm_g, w_branch_a, w_branch_b, w_out, norm2_g, w_ffn_up, ffn_conv_w, ffn_conv_b, w_ffn_down):
    batch, seq, _ = x.shape
    depth = norm1_g.shape[0]
    assert seq % MOBA_BLOCK == 0 and seq % 512 == 0
    t = batch * seq
    row = lambda a: a.reshape(1, -1)
    c3 = 3 * WIDTH
    rwkv_in = c3 + DECAY_LORA + AAA_LORA + GATE_LORA
    hsum = _rwkv_masks()[1]

    x2 = x.reshape(t, D_MODEL)
    for l in range(depth):
        wi = w_in[l]
        wr = wi[:, :c3].astype(BF16)
        wl = _pad_lora_cols(wi[:, c3:rwkv_in]).astype(BF16)
        wm = wi[:, rwkv_in:rwkv_in + c3].astype(BF16)
        wg = wi[:, rwkv_in + c3:].astype(BF16)
        zr, zl, zm, gates = _inproj(x2, row(norm1_g[l]), wr, wl, wm, wg, tm=512)

        consts = [row(rwkv_mu[l][:c3]), _pad_lora_cols(row(rwkv_mu[l][c3:])),
                  _pad_rows(w_decay_up[l], LANES).astype(BF16), row(decay_bias[l]),
                  _pad_rows(w_aaa_up[l], LANES).astype(BF16), row(aaa_bias[l]),
                  _pad_rows(w_gate_up[l], 2 * LANES).astype(BF16),
                  row(rwkv_k_k[l]), row(rwkv_k_a[l]), row(rwkv_r_k[l]),
                  row(rwkv_ln_g[l]), row(rwkv_ln_b[l])]
        ya = _rwkv(zr, zl, consts, batch, seq).reshape(t, WIDTH)

        tile8 = lambda a: row(jnp.tile(a, N_HEADS))
        qa, ka, va = _moba_prep(zm, positions, tile8(q_norm_g[l]), tile8(k_norm_g[l]), hsum, batch, seq)
        yb = _moba_attn(qa, ka, va, batch, seq).reshape(t, WIDTH)

        x1, h2 = _merge(x2, ya, yb, gates, w_branch_a[l].astype(BF16), w_branch_b[l].astype(BF16),
                        w_out[l].astype(BF16), row(norm2_g[l]), tm=512)

        wu = w_ffn_up[l]
        x2 = _ffn(h2, x1, wu[:, :D_FF].astype(BF16), wu[:, D_FF:].astype(BF16), ffn_conv_w[l],
                  row(ffn_conv_b[l]), w_ffn_down[l].astype(BF16), tm=512, seq=seq)
    return x2.reshape(batch, seq, D_MODEL)
```

```python
import functools

import numpy as np
import jax
import jax.numpy as jnp
from jax import lax
from jax.experimental import pallas as pl
from jax.experimental.pallas import tpu as pltpu

F32 = jnp.float32
BF16 = jnp.bfloat16

D_MODEL = 1024
HEAD_DIM = 64
N_HEADS = 8
WIDTH = N_HEADS * HEAD_DIM
DECAY_LORA = 64
AAA_LORA = 64
GATE_LORA = 160
LORA_PAD = 512
MOBA_BLOCK = 256
MOBA_TOPK = 3
ROT_DIM = HEAD_DIM // 4
ROPE_THETA = 500000.0
D_FF = 2816
NORM_EPS = 1e-6
GN_EPS = 64e-5
LANES = 128
CHUNK = 64
NEG_BIG = -32768.0
VMEM_LIMIT = 56 * 1024 * 1024


def _sigmoid(x):
    return 1.0 / (1.0 + jnp.exp(-x))


def _dot(a, b, precision=None):
    return jnp.dot(a, b, preferred_element_type=F32, precision=precision)


def _dot_nt(a, b, precision=None):
    return lax.dot_general(a, b, (((1,), (1,)), ((), ())), preferred_element_type=F32,
                           precision=precision)


def _head_sums(x, bd_ones):
    g = bd_ones.shape[0]
    return jnp.concatenate([_dot(x[:, j:j + g].astype(BF16), bd_ones) for j in range(0, x.shape[1], g)],
                           axis=1)


def _dot_tn(a, b, precision=None):
    return lax.dot_general(a, b, (((0,), (0,)), ((), ())), preferred_element_type=F32,
                           precision=precision)


def _params(*sem):
    return pltpu.CompilerParams(dimension_semantics=sem, vmem_limit_bytes=VMEM_LIMIT)


def _const_spec(shape):
    nd = len(shape)
    return pl.BlockSpec(shape, lambda *_: (0,) * nd, pipeline_mode=pl.Buffered(1))


def _inproj_kernel(x_ref, g_ref, wr_ref, wl_ref, wm_ref, wg_ref,
                   zr_ref, zl_ref, zm_ref, gate_ref):
    half = x_ref.shape[0] // 2
    for rows in (slice(0, half), slice(half, 2 * half)):
        x = x_ref[rows, :]
        ms = jnp.mean(x * x, axis=-1, keepdims=True)
        h = (x * lax.rsqrt(ms + NORM_EPS) * g_ref[...]).astype(BF16)
        zr_ref[rows, :] = _dot(h, wr_ref[...])
        zl_ref[rows, :] = _dot(h, wl_ref[...])
        zm_ref[rows, :] = _dot(h, wm_ref[...])
        gate_ref[rows, :] = _sigmoid(_dot(h, wg_ref[...])).astype(BF16)


def _inproj(x2, g, wr, wl, wm, wg, tm):
    t = x2.shape[0]
    row = lambda n: pl.BlockSpec((tm, n), lambda i: (i, 0))
    return pl.pallas_call(
        _inproj_kernel,
        grid=(t // tm,),
        in_specs=[row(D_MODEL), _const_spec(g.shape), _const_spec(wr.shape), _const_spec(wl.shape),
                  _const_spec(wm.shape), _const_spec(wg.shape)],
        out_specs=[row(3 * WIDTH), row(LORA_PAD), row(3 * WIDTH), row(2 * D_MODEL)],
        out_shape=[jax.ShapeDtypeStruct((t, 3 * WIDTH), F32),
                   jax.ShapeDtypeStruct((t, LORA_PAD), F32),
                   jax.ShapeDtypeStruct((t, 3 * WIDTH), F32),
                   jax.ShapeDtypeStruct((t, 2 * D_MODEL), BF16)],
        compiler_params=_params("parallel"),
        name="inproj",
    )(x2, g, wr, wl, wm, wg)


RW_ROWS = 256
RW_SEQS = 4
RW_GROUP = 256


def _rwkv_kernel(zr_ref, zrp_ref, zl_ref, zlp_ref, mur_ref, mul_ref, wd_ref, db_ref, wa_ref,
                 ab_ref, wg_ref, kk_ref, ka_ref, rk_ref, lng_ref, lnb_ref, cum_ref, bd_ref,
                 o_ref, state_ref):
    c = pl.program_id(1)
    n_seq, TS = zr_ref.shape[0], zr_ref.shape[1]
    TL = n_seq * TS
    L = CHUNK
    G = RW_GROUP

    @pl.when(c == 0)
    def _():
        state_ref[...] = jnp.zeros_like(state_ref)

    def token_shift(z_ref, zp_ref, mu_ref):
        z = z_ref[...].reshape(TL, z_ref.shape[2])
        row = lax.broadcasted_iota(jnp.int32, z.shape, 0)
        zs = pltpu.roll(z, 1, axis=0)
        for s in range(n_seq):
            zs = jnp.where(row == s * TS, jnp.where(c == 0, 0.0, zp_ref[s, 7:8, :]), zs)
        return z + mu_ref[...] * (zs - z)

    zr = token_shift(zr_ref, zrp_ref, mur_ref)
    zl = token_shift(zl_ref, zlp_ref, mul_ref)
    r = zr[:, 0:WIDTH]
    k = zr[:, WIDTH:2 * WIDTH]
    v = zr[:, 2 * WIDTH:3 * WIDTH]
    xw = zl[:, 0:LANES]
    xa = zl[:, LANES:2 * LANES]
    xg = zl[:, 2 * LANES:4 * LANES]

    dd = db_ref[...] + _dot(jnp.tanh(xw).astype(BF16), wd_ref[...])
    w_log = -(jnp.maximum(-dd, 0.0) + jnp.log(1.0 + jnp.exp(-jnp.abs(dd)))) - 0.5
    lw = -jnp.exp(w_log)
    asig = _sigmoid(ab_ref[...] + _dot(xa.astype(BF16), wa_ref[...]))
    g = _dot(_sigmoid(xg).astype(BF16), wg_ref[...])
    bdm = bd_ref[...]
    kkf = k * kk_ref[...]
    kk = kkf * lax.rsqrt(jnp.maximum(_head_sums(kkf * kkf, bdm), 1e-24))
    kmod = k * (1.0 + (asig - 1.0) * ka_ref[...])
    bonus = _head_sums(r * kmod * rk_ref[...], bdm) * v
    b = kk * asig

    lw_hi = lw.astype(BF16)
    lw_lo = (lw - lw_hi.astype(F32)).astype(BF16)
    cum = cum_ref[...]
    cums = [_dot(cum, lw_hi[ci * L:(ci + 1) * L]) + _dot(cum, lw_lo[ci * L:(ci + 1) * L]) for ci in range(TL // L)]
    cw = jnp.concatenate([cs[0:L] for cs in cums], axis=0)
    cw_end = jnp.concatenate([cs[L:2 * L] for cs in cums], axis=0)
    e_neg = jnp.exp(-cw)
    e_end = jnp.exp(cw_end - cw)
    a_til = (-kk * jnp.exp(cw - lw)).astype(BF16)
    r_til = (r * jnp.exp(cw)).astype(BF16)
    b_til = (b * e_neg).astype(BF16)
    k_til = (kmod * e_neg).astype(BF16)
    b_hat = b * e_end
    k_hat = kmod * e_end
    w_end = jnp.exp(cw_end)
    vb = v.astype(BF16)

    def bd(x):
        return jnp.concatenate([x.astype(BF16)] * (G // L), axis=0) * bdm

    lane_lo = lax.broadcasted_iota(jnp.int32, (L, LANES), 1) < HEAD_DIM

    def head_t(x):
        t = jnp.concatenate([x, x], axis=0).T
        return jnp.concatenate([jnp.where(lane_lo, t[2 * j * L:(2 * j + 1) * L], t[(2 * j + 1) * L:(2 * j + 2) * L])
                                for j in range(G // LANES)], axis=1)

    row2 = lax.broadcasted_iota(jnp.int32, (2 * L, G), 0)
    lane_t = lax.broadcasted_iota(jnp.int32, (2 * L, G), 1) % L
    tri2 = lane_t < jnp.where(row2 < L, row2, row2 - L + 1)
    eye = (lax.broadcasted_iota(jnp.int32, (L, G), 1) % L
           == lax.broadcasted_iota(jnp.int32, (L, G), 0)).astype(F32)

    n_chunks, n_groups = TL // L, WIDTH // G
    pairs = [(ci, gi) for ci in range(n_chunks) for gi in range(n_groups)]

    def blk(x, cg):
        ci, gi = cg
        return x[ci * L:(ci + 1) * L, gi * G:(gi + 1) * G]

    p = {cg: jnp.concatenate([blk(a_til, cg), blk(r_til, cg)], axis=0) for cg in pairs}
    ab_rb = {cg: jnp.where(tri2, _dot_nt(p[cg], bd(blk(b_til, cg))), 0.0) for cg in pairs}
    ak_rk = {cg: jnp.where(tri2, _dot_nt(p[cg], bd(blk(k_til, cg))), 0.0) for cg in pairs}
    tinv = {cg: eye + ab_rb[cg][0:L] for cg in pairs}
    pw = {cg: _dot(ab_rb[cg][0:L].astype(BF16), bd(ab_rb[cg][0:L])) for cg in pairs}
    for _ in range(int(np.log2(L)) - 2):
        both = {cg: _dot(jnp.concatenate([pw[cg], tinv[cg]], axis=0).astype(BF16), bd(pw[cg])) for cg in pairs}
        pw = {cg: both[cg][0:L] for cg in pairs}
        tinv = {cg: tinv[cg] + both[cg][L:2 * L] for cg in pairs}
    tinv = {cg: (tinv[cg] + _dot(tinv[cg].astype(BF16), bd(pw[cg]))).astype(BF16) for cg in pairs}
    bd_v = {cg: bd(blk(vb, cg)) for cg in pairs}
    kv = {cg: _dot(ak_rk[cg].astype(BF16), bd_v[cg]) for cg in pairs}
    kt_v = {cg: _dot(head_t(blk(k_hat, cg)).astype(BF16), bd_v[cg]) for cg in pairs}
    arb_bt = {cg: jnp.concatenate([ab_rb[cg][L:2 * L], head_t(blk(b_hat, cg))], axis=0).astype(BF16)
              for cg in pairs}
    w_row = {cg: head_t(blk(w_end, cg)) for cg in pairs}

    chains = [(s, gi) for s in range(n_seq) for gi in range(n_groups)]
    seq_chunks = TS // L
    st = {sg: state_ref[sg[0] * n_groups + sg[1]] for sg in chains}
    y_blk = {}
    for ci in range(seq_chunks):
        cg = {sg: (sg[0] * seq_chunks + ci, sg[1]) for sg in chains}
        ps = {sg: _dot(p[cg[sg]], bd(st[sg])) for sg in chains}
        u = {sg: _dot(tinv[cg[sg]], bd(ps[sg][0:L] + kv[cg[sg]][0:L])) for sg in chains}
        yu = {sg: _dot(arb_bt[cg[sg]], bd(u[sg])) for sg in chains}
        st = {sg: w_row[cg[sg]] * st[sg] + yu[sg][L:2 * L] + kt_v[cg[sg]] for sg in chains}
        for sg in chains:
            y_blk[cg[sg]] = ps[sg][L:2 * L] + kv[cg[sg]][L:2 * L] + yu[sg][0:L]
    for sg in chains:
        state_ref[sg[0] * n_groups + sg[1]] = st[sg]
    y = jnp.concatenate([jnp.concatenate([y_blk[(ci, gi)] for gi in range(n_groups)], axis=1)
                         for ci in range(n_chunks)], axis=0)

    inv_n = 1.0 / HEAD_DIM
    mu = _head_sums(y, bdm) * inv_n
    yc = y - mu
    var = _head_sums(yc * yc, bdm) * inv_n
    yn = yc * lax.rsqrt(var + GN_EPS) * lng_ref[...] + lnb_ref[...]
    o_ref[...] = ((yn + bonus) * g).astype(o_ref.dtype).reshape(o_ref.shape)


def _rwkv_masks():
    t = np.arange(CHUNK)
    tri = t[:, None] >= t[None, :]
    cum = np.concatenate([tri, np.ones_like(tri)], axis=0)
    g = np.arange(RW_GROUP)
    bd = (g[:, None] // HEAD_DIM) == (g[None, :] // HEAD_DIM)
    return [jnp.asarray(m, BF16) for m in (cum, bd)]


def _rwkv(zr, zl, consts, batch, seq):
    TL = RW_ROWS
    nt = seq // TL
    ns = RW_SEQS if batch % RW_SEQS == 0 else 1
    cur = lambda n: pl.BlockSpec((ns, TL, n), lambda b, c: (b, c, 0))
    prv = lambda n: pl.BlockSpec((ns, 8, n), lambda b, c: (b, jnp.maximum(c * (TL // 8) - 1, 0), 0))
    zr3 = zr.reshape(batch, seq, 3 * WIDTH)
    zl3 = zl.reshape(batch, seq, LORA_PAD)
    consts = list(consts) + _rwkv_masks()
    return pl.pallas_call(
        _rwkv_kernel,
        grid=(batch // ns, nt),
        in_specs=[cur(3 * WIDTH), prv(3 * WIDTH), cur(LORA_PAD), prv(LORA_PAD)]
                 + [_const_spec(a.shape) for a in consts],
        out_specs=pl.BlockSpec((ns, TL, WIDTH), lambda b, c: (b, c, 0)),
        out_shape=jax.ShapeDtypeStruct((batch, seq, WIDTH), BF16),
        scratch_shapes=[pltpu.VMEM((ns * (WIDTH // RW_GROUP), HEAD_DIM, RW_GROUP), F32)],
        compiler_params=_params("parallel", "arbitrary"),
        name="rwkv",
    )(zr3, zr3, zl3, zl3, *consts)


QK_SCALE_LOG2 = HEAD_DIM ** -0.5 * float(np.log2(np.e))
AUX_BLOCKS = LANES // N_HEADS
HALF_HEADS = N_HEADS // 2
VT_ROWS = 80


def _moba_prep_kernel(zm_ref, pos_ref, invf_ref, ropep_ref, qg_ref, kg_ref, hsum_ref,
                      q_out, k_out, v_out, km_ref, *, n_blocks):
    i = pl.program_id(1)
    TB = zm_ref.shape[0]

    @pl.when(i == 0)
    def _():
        km_ref[...] = jnp.zeros_like(km_ref)

    zm = zm_ref[...]
    ang = invf_ref[...] * pos_ref[...].astype(F32)
    cs = jnp.concatenate([jnp.cos(ang), jnp.sin(ang)], axis=0)
    cs_hi = cs.astype(BF16).astype(F32)
    tab = _dot_tn(jnp.concatenate([cs_hi, cs - cs_hi], axis=0), ropep_ref[...])
    lane = lax.broadcasted_iota(jnp.int32, (TB, LANES), 1)
    rep = WIDTH // LANES
    half = ROT_DIM // 2
    cos = jnp.concatenate([tab[:, 0:LANES] + jnp.where(lane % HEAD_DIM >= ROT_DIM, 1.0, 0.0)] * rep, axis=-1)
    sin_lo = jnp.concatenate([tab[:, LANES:2 * LANES]] * rep, axis=-1)
    sin_hi = jnp.concatenate([tab[:, 2 * LANES:3 * LANES]] * rep, axis=-1)
    hsum = hsum_ref[...]

    def norm_rope(t, gain):
        ms = _head_sums(t * t, hsum) * (1.0 / HEAD_DIM)
        t = t * lax.rsqrt(ms + NORM_EPS) * gain
        return (t * cos + pltpu.roll(t, WIDTH - half, axis=1) * sin_lo
                + pltpu.roll(t, half, axis=1) * sin_hi)

    q = norm_rope(zm[:, 0:WIDTH], qg_ref[...])
    k = norm_rope(zm[:, WIDTH:2 * WIDTH], kg_ref[...])
    v = zm[:, 2 * WIDTH:3 * WIDTH]

    km = km_ref[...]
    km_hi, q_hi = km.astype(BF16), q.astype(BF16)
    km_lo, q_lo = (km - km_hi.astype(F32)).astype(BF16), (q - q_hi.astype(F32)).astype(BF16)
    gate2 = _dot_nt(jnp.concatenate([km_hi, km_lo], axis=0), q_hi)
    gate = (gate2[0:LANES] + gate2[LANES:2 * LANES] + _dot_nt(km_hi, q_lo)).reshape(N_HEADS, AUX_BLOCKS, TB)
    n_idx = lax.broadcasted_iota(jnp.int32, gate.shape, 1)
    gsel = jnp.where(n_idx < i, gate, -jnp.inf)
    picked = n_idx < 0
    for _ in range(MOBA_TOPK):
        m = jnp.max(gsel, axis=1, keepdims=True)
        idx = jnp.min(jnp.where(gsel == m, n_idx, AUX_BLOCKS), axis=1, keepdims=True)
        pick = n_idx == idx
        picked = picked | (pick & (m > -jnp.inf))
        gsel = jnp.where(pick, -jnp.inf, gsel)
    keep = picked | (n_idx == i) | (n_idx >= n_blocks)
    bias = jnp.where(keep, 0.0, NEG_BIG).reshape(LANES, TB).T
    bias_by_half = [pltpu.roll(bias, HEAD_DIM, axis=1), bias]

    rowk = lax.broadcasted_iota(jnp.int32, km.shape, 0)
    lanek = lax.broadcasted_iota(jnp.int32, km.shape, 1)
    mine = (rowk % AUX_BLOCKS == i) & (lanek // HEAD_DIM == rowk // AUX_BLOCKS)
    km_ref[...] = jnp.where(mine, jnp.mean(k, axis=0, keepdims=True), km)

    ones_rows = (lax.broadcasted_iota(jnp.int32, (VT_ROWS - HEAD_DIM, TB), 0) == 0).astype(F32)
    for pr in range(N_HEADS // 2):
        ps = slice(pr * LANES, (pr + 1) * LANES)
        qp, kp = q[:, ps] * QK_SCALE_LOG2, k[:, ps]
        vt = v[:, ps].T
        for odd in range(2):
            h = 2 * pr + odd
            is_data = (lane >= HEAD_DIM) if odd else (lane < HEAD_DIM)
            aux_base = 0 if odd else HEAD_DIM
            upper = h // HALF_HEADS
            own_lane = aux_base + (h % HALF_HEADS) * AUX_BLOCKS + i
            q_out[h] = jnp.where(is_data, qp, bias_by_half[1 - upper if odd else upper]).astype(BF16)
            k_out[h] = jnp.where(is_data, kp, jnp.where(lane == own_lane, 1.0, 0.0)).astype(BF16)
            v_out[h] = jnp.concatenate([vt[odd * HEAD_DIM:(odd + 1) * HEAD_DIM], ones_rows], axis=0).astype(BF16)


def _rope_tables():
    half = ROT_DIM // 2
    invf = ROPE_THETA ** (-np.arange(half, dtype=np.float64) / half)
    lane = np.arange(LANES) % HEAD_DIM
    f = np.arange(half)[:, None]
    p_cos = ((lane[None, :] < ROT_DIM) & (lane[None, :] % half == f)).astype(np.float64)
    p_lo = -(lane[None, :] == f).astype(np.float64)
    p_hi = (lane[None, :] == f + half).astype(np.float64)
    z = np.zeros_like(p_cos)
    cos_rows = np.concatenate([p_cos, z, z], axis=1)
    sin_rows = np.concatenate([z, p_lo, p_hi], axis=1)
    expand = np.concatenate([cos_rows, sin_rows, cos_rows, sin_rows], axis=0)
    return (jnp.asarray(np.broadcast_to(invf[:, None], (half, MOBA_BLOCK)), F32), jnp.asarray(expand, F32))


def _moba_prep(zm, positions, qg, kg, hsum, batch, seq):
    TB = MOBA_BLOCK
    nb = seq // TB
    assert nb <= AUX_BLOCKS
    zm3 = zm.reshape(batch, seq, 3 * WIDTH)
    pos4 = positions.reshape(batch, nb, 1, TB)
    invf, expand = _rope_tables()
    aug = pl.BlockSpec((None, N_HEADS, TB, LANES), lambda b, i: (b, 0, i, 0))
    aug_shape = jax.ShapeDtypeStruct((batch, N_HEADS, seq, LANES), BF16)
    return pl.pallas_call(
        functools.partial(_moba_prep_kernel, n_blocks=nb),
        grid=(batch, nb),
        in_specs=[pl.BlockSpec((None, TB, 3 * WIDTH), lambda b, i: (b, i, 0)),
                  pl.BlockSpec((None, None, 1, TB), lambda b, i: (b, i, 0, 0)),
                  _const_spec(invf.shape), _const_spec(expand.shape), _const_spec(qg.shape),
                  _const_spec(kg.shape), _const_spec(hsum.shape)],
        out_specs=[aug, aug, pl.BlockSpec((None, N_HEADS, None, VT_ROWS, TB), lambda b, i: (b, 0, i, 0, 0))],
        out_shape=[aug_shape, aug_shape, jax.ShapeDtypeStruct((batch, N_HEADS, nb, VT_ROWS, TB), BF16)],
        scratch_shapes=[pltpu.VMEM((LANES, WIDTH), F32)],
        compiler_params=_params("parallel", "arbitrary"),
        name="moba_prep",
    )(zm3, pos4, invf, expand, qg, kg, hsum)


def _moba_attn_kernel(q_ref, k_ref, v_ref, o_ref, acc_ref, m_ref, s_ref, p_ref, a_ref, mx_ref):
    i = pl.program_id(2)
    TB = q_ref.shape[1]
    HP = q_ref.shape[0]
    n_steps = i + 1

    def block_of(tau):
        return jnp.where(tau == 0, i, jnp.minimum(tau - 1, jnp.maximum(i - 1, 0)))

    def qk(tau, par, masked=False):
        koff = pl.multiple_of(block_of(tau) * TB, TB)
        for hh in range(HP):
            s = _dot_nt(k_ref[hh, pl.ds(koff, TB), :], q_ref[hh])
            if masked:
                key = lax.broadcasted_iota(jnp.int32, (TB, TB), 0)
                qry = lax.broadcasted_iota(jnp.int32, (TB, TB), 1)
                s = jnp.where(key <= qry, s, -1e30)
            s_ref[par, hh] = s
            mx_ref[par, hh] = jnp.max(s, axis=0, keepdims=True)

    def softmax(par):
        for hh in range(HP):
            m_old = m_ref[hh]
            m_new = jnp.maximum(m_old, mx_ref[par, hh])
            m_ref[hh] = m_new
            a_ref[par, hh] = jnp.exp2(m_old - m_new)
            p_ref[par, hh] = jnp.exp2(s_ref[par, hh] - m_new).astype(BF16)

    def pv(tau, par):
        live = (tau < n_steps).astype(BF16)
        blk = block_of(tau)
        for hh in range(HP):
            rescale = jnp.where(tau < n_steps, a_ref[par, hh], 1.0)
            acc_ref[hh] = acc_ref[hh] * rescale + _dot(v_ref[hh, blk] * live, p_ref[par, hh])

    for hh in range(HP):
        m_ref[hh] = jnp.full((1, TB), -1e30, F32)
        acc_ref[hh] = jnp.zeros((VT_ROWS, TB), F32)
    qk(0, 0, masked=True)
    qk(1, 1)
    softmax(0)

    def slot_pair(pp, carry):
        t = 2 * pp
        qk(t + 2, 0)
        pv(t, 0)
        softmax(1)
        qk(t + 3, 1)
        pv(t + 1, 1)
        softmax(0)
        return carry

    lax.fori_loop(0, (n_steps + 1) // 2, slot_pair, 0)

    outs = []
    for hh in range(HP):
        acc = acc_ref[hh]
        out_t = jnp.concatenate([acc / acc[HEAD_DIM:HEAD_DIM + 1, :], jnp.zeros((LANES - VT_ROWS, TB), F32)],
                                axis=0)
        outs.append(out_t.T)
    lane = lax.broadcasted_iota(jnp.int32, (TB, LANES), 1)
    for pr in range(HP // 2):
        o_ref[:, pr * LANES:(pr + 1) * LANES] = jnp.where(
            lane < HEAD_DIM, outs[2 * pr], pltpu.roll(outs[2 * pr + 1], HEAD_DIM, axis=1)).astype(o_ref.dtype)


ATTN_HEADS = 8


def _moba_attn(qa, ka, va, batch, seq):
    TB = MOBA_BLOCK
    nb = seq // TB
    HP = ATTN_HEADS
    return pl.pallas_call(
        _moba_attn_kernel,
        grid=(batch, N_HEADS // HP, nb),
        in_specs=[pl.BlockSpec((None, HP, TB, LANES), lambda b, p, i: (b, p, i, 0)),
                  pl.BlockSpec((None, HP, seq, LANES), lambda b, p, i: (b, p, 0, 0)),
                  pl.BlockSpec((None, HP, nb, VT_ROWS, TB), lambda b, p, i: (b, p, 0, 0, 0))],
        out_specs=pl.BlockSpec((None, TB, HP * HEAD_DIM), lambda b, p, i: (b, i, p)),
        out_shape=jax.ShapeDtypeStruct((batch, seq, WIDTH), BF16),
        scratch_shapes=[pltpu.VMEM((HP, VT_ROWS, TB), F32), pltpu.VMEM((HP, 1, TB), F32),
                        pltpu.VMEM((2, HP, TB, TB), F32), pltpu.VMEM((2, HP, TB, TB), BF16),
                        pltpu.VMEM((2, HP, 1, TB), F32), pltpu.VMEM((2, HP, 1, TB), F32)],
        compiler_params=_params("parallel", "parallel", "arbitrary"),
        name="moba_attn",
    )(qa, ka, va)


def _merge_kernel(x_ref, ya_ref, yb_ref, gate_ref, wa_ref, wb_ref, wo_ref, g2_ref, x1_ref, h2_ref):
    half = x_ref.shape[0] // 2
    halves = (slice(0, half), slice(half, 2 * half))
    ua = [_dot(ya_ref[rows, :], wa_ref[...]) for rows in halves]
    ub = [_dot(yb_ref[rows, :], wb_ref[...]) for rows in halves]
    for hi, rows in enumerate(halves):
        gate = gate_ref[rows, :].astype(F32)
        mix = (gate[:, 0:D_MODEL] * ua[hi] + gate[:, D_MODEL:] * ub[hi]).astype(BF16)
        x1 = x_ref[rows, :] + _dot(mix, wo_ref[...])
        x1_ref[rows, :] = x1
        ms = jnp.mean(x1 * x1, axis=-1, keepdims=True)
        h2_ref[rows, :] = (x1 * lax.rsqrt(ms + NORM_EPS) * g2_ref[...]).astype(BF16)


def _merge(x2, ya, yb, gates, wa, wb, wo, g2, tm):
    t = x2.shape[0]
    row = lambda n: pl.BlockSpec((tm, n), lambda i: (i, 0))
    return pl.pallas_call(
        _merge_kernel,
        grid=(t // tm,),
        in_specs=[row(D_MODEL), row(WIDTH), row(WIDTH), row(2 * D_MODEL), _const_spec(wa.shape),
                  _const_spec(wb.shape), _const_spec(wo.shape), _const_spec(g2.shape)],
        out_specs=[row(D_MODEL), row(D_MODEL)],
        out_shape=[jax.ShapeDtypeStruct((t, D_MODEL), F32), jax.ShapeDtypeStruct((t, D_MODEL), BF16)],
        compiler_params=_params("parallel"),
        name="merge",
    )(x2, ya, yb, gates, wa, wb, wo, g2)


FFN_HALO = 16
FFN_COLS = (768, 768, 768, 512)


def _ffn_kernel(h_ref, hp_ref, x1_ref, wua_ref, wub_ref, cw_ref, cb_ref, wd_ref, o_ref, *, tiles_per_seq):
    i = pl.program_id(0)
    tm = h_ref.shape[0]
    h = h_ref[...]
    halo = jnp.where(i % tiles_per_seq == 0, jnp.zeros_like(hp_ref[...]), hp_ref[...])
    h_ext = jnp.concatenate([halo, h], axis=0)
    starts = np.cumsum((0,) + FFN_COLS)
    groups = [slice(int(starts[g]), int(starts[g + 1])) for g in range(len(FFN_COLS))]

    def up(cs):
        return _dot(h_ext, wua_ref[:, cs]), _dot(h, wub_ref[:, cs])

    acc = x1_ref[...]
    nxt = up(groups[0])
    for g, cs in enumerate(groups):
        a, b = nxt
        if g + 1 < len(groups):
            nxt = up(groups[g + 1])
        conv = (a[FFN_HALO:, :] * cw_ref[2:3, cs] + a[FFN_HALO - 1:FFN_HALO - 1 + tm, :] * cw_ref[1:2, cs]
                + a[FFN_HALO - 2:FFN_HALO - 2 + tm, :] * cw_ref[0:1, cs] + cb_ref[:, cs])
        gelu = 0.5 * conv * (1.0 + lax.erf(conv * (2.0 ** -0.5)))
        acc = acc + _dot((gelu * b).astype(BF16), wd_ref[cs, :])
    o_ref[...] = acc


def _ffn(h2, x1, wua, wub, cw, cb, wd, tm, seq):
    t = h2.shape[0]
    row = lambda n: pl.BlockSpec((tm, n), lambda i: (i, 0))
    halo = pl.BlockSpec((FFN_HALO, D_MODEL), lambda i: (jnp.maximum(i * (tm // FFN_HALO) - 1, 0), 0))
    return pl.pallas_call(
        functools.partial(_ffn_kernel, tiles_per_seq=seq // tm),
        grid=(t // tm,),
        in_specs=[row(D_MODEL), halo, row(D_MODEL), _const_spec(wua.shape), _const_spec(wub.shape),
                  _const_spec(cw.shape), _const_spec(cb.shape), _const_spec(wd.shape)],
        out_specs=row(D_MODEL),
        out_shape=jax.ShapeDtypeStruct((t, D_MODEL), F32),
        compiler_params=_params("parallel"),
        name="ffn",
    )(h2, h2, x1, wua, wub, cw, cb, wd)


def _pad_lora_cols(a):
    z = lambda n: jnp.zeros(a.shape[:-1] + (n,), a.dtype)
    o1, o2 = DECAY_LORA, DECAY_LORA + AAA_LORA
    return jnp.concatenate([a[..., :o1], z(LANES - DECAY_LORA), a[..., o1:o2], z(LANES - AAA_LORA),
                            a[..., o2:], z(2 * LANES - GATE_LORA)], axis=-1)


def _pad_rows(a, n):
    return jnp.concatenate([a, jnp.zeros((n - a.shape[0],) + a.shape[1:], a.dtype)], axis=0)


def kernel(x, positions, norm1_g, w_in, rwkv_mu, w_decay_up, decay_bias, w_aaa_up, aaa_bias, w_gate_up, rwkv_k_k, rwkv_k_a, rwkv_r_k, rwkv_ln_g, rwkv_ln_b, q_norm_g, k_norm_g, w_branch_a, w_branch_b, w_out, norm2_g, w_ffn_up, ffn_conv_w, ffn_conv_b, w_ffn_down):
    batch, seq, _ = x.shape
    depth = norm1_g.shape[0]
    assert seq % MOBA_BLOCK == 0 and seq % 512 == 0
    t = batch * seq
    row = lambda a: a.reshape(1, -1)
    c3 = 3 * WIDTH
    rwkv_in = c3 + DECAY_LORA + AAA_LORA + GATE_LORA
    hsum = _rwkv_masks()[1]

    x2 = x.reshape(t, D_MODEL)
    for l in range(depth):
        wi = w_in[l]
        wr = wi[:, :c3].astype(BF16)
        wl = _pad_lora_cols(wi[:, c3:rwkv_in]).astype(BF16)
        wm = wi[:, rwkv_in:rwkv_in + c3].astype(BF16)
        wg = wi[:, rwkv_in + c3:].astype(BF16)
        zr, zl, zm, gates = _inproj(x2, row(norm1_g[l]), wr, wl, wm, wg, tm=512)

        consts = [row(rwkv_mu[l][:c3]), _pad_lora_cols(row(rwkv_mu[l][c3:])),
                  _pad_rows(w_decay_up[l], LANES).astype(BF16), row(decay_bias[l]),
                  _pad_rows(w_aaa_up[l], LANES).astype(BF16), row(aaa_bias[l]),
                  _pad_rows(w_gate_up[l], 2 * LANES).astype(BF16),
                  row(rwkv_k_k[l]), row(rwkv_k_a[l]), row(rwkv_r_k[l]),
                  row(rwkv_ln_g[l]), row(rwkv_ln_b[l])]
        ya = _rwkv(zr, zl, consts, batch, seq).reshape(t, WIDTH)

        tile8 = lambda a: row(jnp.tile(a, N_HEADS))
        qa, ka, va = _moba_prep(zm, positions, tile8(q_norm_g[l]), tile8(k_norm_g[l]), hsum, batch, seq)
        yb = _moba_attn(qa, ka, va, batch, seq).reshape(t, WIDTH)

        x1, h2 = _merge(x2, ya, yb, gates, w_branch_a[l].astype(BF16), w_branch_b[l].astype(BF16),
                        w_out[l].astype(BF16), row(norm2_g[l]), tm=512)

        wu = w_ffn_up[l]
        x2 = _ffn(h2, x1, wu[:, :D_FF].astype(BF16), wu[:, D_FF:].astype(BF16), ffn_conv_w[l],
                  row(ffn_conv_b[l]), w_ffn_down[l].astype(BF16), tm=512, seq=seq)
    return x2.reshape(batch, seq, D_MODEL)
```

```python
import functools

import numpy as np
import jax
import jax.numpy as jnp
from jax import lax
from jax.experimental import pallas as pl
from jax.experimental.pallas import tpu as pltpu

F32 = jnp.float32
BF16 = jnp.bfloat16

D_MODEL = 1024
HEAD_DIM = 64
N_HEADS = 8
WIDTH = N_HEADS * HEAD_DIM
DECAY_LORA = 64
AAA_LORA = 64
GATE_LORA = 160
LORA_PAD = 512
MOBA_BLOCK = 256
MOBA_TOPK = 3
ROT_DIM = HEAD_DIM // 4
ROPE_THETA = 500000.0
D_FF = 2816
NORM_EPS = 1e-6
GN_EPS = 64e-5
LANES = 128
CHUNK = 64
NEG_BIG = -32768.0
VMEM_LIMIT = 56 * 1024 * 1024


def _sigmoid(x):
    return 1.0 / (1.0 + jnp.exp(-x))


def _dot(a, b, precision=None):
    return jnp.dot(a, b, preferred_element_type=F32, precision=precision)


def _dot_nt(a, b, precision=None):
    return lax.dot_general(a, b, (((1,), (1,)), ((), ())), preferred_element_type=F32,
                           precision=precision)


def _head_sums(x, bd_ones):
    g = bd_ones.shape[0]
    return jnp.concatenate([_dot(x[:, j:j + g].astype(BF16), bd_ones) for j in range(0, x.shape[1], g)],
                           axis=1)


def _dot_tn(a, b, precision=None):
    return lax.dot_general(a, b, (((0,), (0,)), ((), ())), preferred_element_type=F32,
                           precision=precision)


def _params(*sem):
    return pltpu.CompilerParams(dimension_semantics=sem, vmem_limit_bytes=VMEM_LIMIT)


def _const_spec(shape):
    nd = len(shape)
    return pl.BlockSpec(shape, lambda *_: (0,) * nd, pipeline_mode=pl.Buffered(1))


def _inproj_kernel(x_ref, g_ref, wr_ref, wl_ref, wm_ref, wg_ref,
                   zr_ref, zl_ref, zm_ref, gate_ref):
    half = x_ref.shape[0] // 2
    for rows in (slice(0, half), slice(half, 2 * half)):
        x = x_ref[rows, :]
        ms = jnp.mean(x * x, axis=-1, keepdims=True)
        h = (x * lax.rsqrt(ms + NORM_EPS) * g_ref[...]).astype(BF16)
        zr_ref[rows, :] = _dot(h, wr_ref[...])
        zl_ref[rows, :] = _dot(h, wl_ref[...])
        zm_ref[rows, :] = _dot(h, wm_ref[...])
        gate_ref[rows, :] = _sigmoid(_dot(h, wg_ref[...])).astype(BF16)


def _inproj(x2, g, wr, wl, wm, wg, tm):
    t = x2.shape[0]
    row = lambda n: pl.BlockSpec((tm, n), lambda i: (i, 0))
    return pl.pallas_call(
        _inproj_kernel,
        grid=(t // tm,),
        in_specs=[row(D_MODEL), _const_spec(g.shape), _const_spec(wr.shape), _const_spec(wl.shape),
                  _const_spec(wm.shape), _const_spec(wg.shape)],
        out_specs=[row(3 * WIDTH), row(LORA_PAD), row(3 * WIDTH), row(2 * D_MODEL)],
        out_shape=[jax.ShapeDtypeStruct((t, 3 * WIDTH), F32),
                   jax.ShapeDtypeStruct((t, LORA_PAD), F32),
                   jax.ShapeDtypeStruct((t, 3 * WIDTH), F32),
                   jax.ShapeDtypeStruct((t, 2 * D_MODEL), BF16)],
        compiler_params=_params("parallel"),
        name="inproj",
    )(x2, g, wr, wl, wm, wg)


RW_ROWS = 256
RW_SEQS = 4
RW_GROUP = 256


def _rwkv_kernel(zr_ref, zrp_ref, zl_ref, zlp_ref, mur_ref, mul_ref, wd_ref, db_ref, wa_ref,
                 ab_ref, wg_ref, kk_ref, ka_ref, rk_ref, lng_ref, lnb_ref, cum_ref, bd_ref,
                 o_ref, state_ref):
    c = pl.program_id(1)
    n_seq, TS = zr_ref.shape[0], zr_ref.shape[1]
    TL = n_seq * TS
    L = CHUNK
    G = RW_GROUP

    @pl.when(c == 0)
    def _():
        state_ref[...] = jnp.zeros_like(state_ref)

    def token_shift(z_ref, zp_ref, mu_ref):
        z = z_ref[...].reshape(TL, z_ref.shape[2])
        row = lax.broadcasted_iota(jnp.int32, z.shape, 0)
        zs = pltpu.roll(z, 1, axis=0)
        for s in range(n_seq):
            zs = jnp.where(row == s * TS, jnp.where(c == 0, 0.0, zp_ref[s, 7:8, :]), zs)
        return z + mu_ref[...] * (zs - z)

    zr = token_shift(zr_ref, zrp_ref, mur_ref)
    zl = token_shift(zl_ref, zlp_ref, mul_ref)
    r = zr[:, 0:WIDTH]
    k = zr[:, WIDTH:2 * WIDTH]
    v = zr[:, 2 * WIDTH:3 * WIDTH]
    xw = zl[:, 0:LANES]
    xa = zl[:, LANES:2 * LANES]
    xg = zl[:, 2 * LANES:4 * LANES]

    dd = db_ref[...] + _dot(jnp.tanh(xw).astype(BF16), wd_ref[...])
    w_log = -(jnp.maximum(-dd, 0.0) + jnp.log(1.0 + jnp.exp(-jnp.abs(dd)))) - 0.5
    lw = -jnp.exp(w_log)
    asig = _sigmoid(ab_ref[...] + _dot(xa.astype(BF16), wa_ref[...]))
    g = _dot(_sigmoid(xg).astype(BF16), wg_ref[...])
    bdm = bd_ref[...]
    kkf = k * kk_ref[...]
    kk = kkf * lax.rsqrt(jnp.maximum(_head_sums(kkf * kkf, bdm), 1e-24))
    kmod = k * (1.0 + (asig - 1.0) * ka_ref[...])
    bonus = _head_sums(r * kmod * rk_ref[...], bdm) * v
    b = kk * asig

    lw_hi = lw.astype(BF16)
    lw_lo = (lw - lw_hi.astype(F32)).astype(BF16)
    cum = cum_ref[...]
    cums = [_dot(cum, lw_hi[ci * L:(ci + 1) * L]) + _dot(cum, lw_lo[ci * L:(ci + 1) * L]) for ci in range(TL // L)]
    cw = jnp.concatenate([cs[0:L] for cs in cums], axis=0)
    cw_end = jnp.concatenate([cs[L:2 * L] for cs in cums], axis=0)
    e_neg = jnp.exp(-cw)
    e_end = jnp.exp(cw_end - cw)
    a_til = (-kk * jnp.exp(cw - lw)).astype(BF16)
    r_til = (r * jnp.exp(cw)).astype(BF16)
    b_til = (b * e_neg).astype(BF16)
    k_til = (kmod * e_neg).astype(BF16)
    b_hat = b * e_end
    k_hat = kmod * e_end
    w_end = jnp.exp(cw_end)
    vb = v.astype(BF16)

    def bd(x):
        return jnp.concatenate([x.astype(BF16)] * (G // L), axis=0) * bdm

    lane_lo = lax.broadcasted_iota(jnp.int32, (L, LANES), 1) < HEAD_DIM

    def head_t(x):
        t = jnp.concatenate([x, x], axis=0).T
        return jnp.concatenate([jnp.where(lane_lo, t[2 * j * L:(2 * j + 1) * L], t[(2 * j + 1) * L:(2 * j + 2) * L])
                                for j in range(G // LANES)], axis=1)

    row2 = lax.broadcasted_iota(jnp.int32, (2 * L, G), 0)
    lane_t = lax.broadcasted_iota(jnp.int32, (2 * L, G), 1) % L
    tri2 = lane_t < jnp.where(row2 < L, row2, row2 - L + 1)
    eye = (lax.broadcasted_iota(jnp.int32, (L, G), 1) % L
           == lax.broadcasted_iota(jnp.int32, (L, G), 0)).astype(F32)

    n_chunks, n_groups = TL // L, WIDTH // G
    pairs = [(ci, gi) for ci in range(n_chunks) for gi in range(n_groups)]

    def blk(x, cg):
        ci, gi = cg
        return x[ci * L:(ci + 1) * L, gi * G:(gi + 1) * G]

    p = {cg: jnp.concatenate([blk(a_til, cg), blk(r_til, cg)], axis=0) for cg in pairs}
    ab_rb = {cg: jnp.where(tri2, _dot_nt(p[cg], bd(blk(b_til, cg))), 0.0) for cg in pairs}
    ak_rk = {cg: jnp.where(tri2, _dot_nt(p[cg], bd(blk(k_til, cg))), 0.0) for cg in pairs}
    tinv = {cg: eye + ab_rb[cg][0:L] for cg in pairs}
    pw = {cg: _dot(ab_rb[cg][0:L].astype(BF16), bd(ab_rb[cg][0:L])) for cg in pairs}
    for _ in range(int(np.log2(L)) - 2):
        both = {cg: _dot(jnp.concatenate([pw[cg], tinv[cg]], axis=0).astype(BF16), bd(pw[cg])) for cg in pairs}
        pw = {cg: both[cg][0:L] for cg in pairs}
        tinv = {cg: tinv[cg] + both[cg][L:2 * L] for cg in pairs}
    tinv = {cg: (tinv[cg] + _dot(tinv[cg].astype(BF16), bd(pw[cg]))).astype(BF16) for cg in pairs}
    bd_v = {cg: bd(blk(vb, cg)) for cg in pairs}
    kv = {cg: _dot(ak_rk[cg].astype(BF16), bd_v[cg]) for cg in pairs}
    kt_v = {cg: _dot(head_t(blk(k_hat, cg)).astype(BF16), bd_v[cg]) for cg in pairs}
    arb_bt = {cg: jnp.concatenate([ab_rb[cg][L:2 * L], head_t(blk(b_hat, cg))], axis=0).astype(BF16)
              for cg in pairs}
    w_row = {cg: head_t(blk(w_end, cg)) for cg in pairs}

    chains = [(s, gi) for s in range(n_seq) for gi in range(n_groups)]
    seq_chunks = TS // L
    st = {sg: state_ref[sg[0] * n_groups + sg[1]] for sg in chains}
    y_blk = {}
    for ci in range(seq_chunks):
        cg = {sg: (sg[0] * seq_chunks + ci, sg[1]) for sg in chains}
        ps = {sg: _dot(p[cg[sg]], bd(st[sg])) for sg in chains}
        u = {sg: _dot(tinv[cg[sg]], bd(ps[sg][0:L] + kv[cg[sg]][0:L])) for sg in chains}
        yu = {sg: _dot(arb_bt[cg[sg]], bd(u[sg])) for sg in chains}
        st = {sg: w_row[cg[sg]] * st[sg] + yu[sg][L:2 * L] + kt_v[cg[sg]] for sg in chains}
        for sg in chains:
            y_blk[cg[sg]] = ps[sg][L:2 * L] + kv[cg[sg]][L:2 * L] + yu[sg][0:L]
    for sg in chains:
        state_ref[sg[0] * n_groups + sg[1]] = st[sg]
    y = jnp.concatenate([jnp.concatenate([y_blk[(ci, gi)] for gi in range(n_groups)], axis=1)
                         for ci in range(n_chunks)], axis=0)

    inv_n = 1.0 / HEAD_DIM
    mu = _head_sums(y, bdm) * inv_n
    yc = y - mu
    var = _head_sums(yc * yc, bdm) * inv_n
    yn = yc * lax.rsqrt(var + GN_EPS) * lng_ref[...] + lnb_ref[...]
    o_ref[...] = ((yn + bonus) * g).astype(o_ref.dtype).reshape(o_ref.shape)


def _rwkv_masks():
    t = np.arange(CHUNK)
    tri = t[:, None] >= t[None, :]
    cum = np.concatenate([tri, np.ones_like(tri)], axis=0)
    g = np.arange(RW_GROUP)
    bd = (g[:, None] // HEAD_DIM) == (g[None, :] // HEAD_DIM)
    return [jnp.asarray(m, BF16) for m in (cum, bd)]


def _rwkv(zr, zl, consts, batch, seq):
    TL = RW_ROWS
    nt = seq // TL
    ns = RW_SEQS if batch % RW_SEQS == 0 else 1
    cur = lambda n: pl.BlockSpec((ns, TL, n), lambda b, c: (b, c, 0))
    prv = lambda n: pl.BlockSpec((ns, 8, n), lambda b, c: (b, jnp.maximum(c * (TL // 8) - 1, 0), 0))
    zr3 = zr.reshape(batch, seq, 3 * WIDTH)
    zl3 = zl.reshape(batch, seq, LORA_PAD)
    consts = list(consts) + _rwkv_masks()
    return pl.pallas_call(
        _rwkv_kernel,
        grid=(batch // ns, nt),
        in_specs=[cur(3 * WIDTH), prv(3 * WIDTH), cur(LORA_PAD), prv(LORA_PAD)]
                 + [_const_spec(a.shape) for a in consts],
        out_specs=pl.BlockSpec((ns, TL, WIDTH), lambda b, c: (b, c, 0)),
        out_shape=jax.ShapeDtypeStruct((batch, seq, WIDTH), BF16),
        scratch_shapes=[pltpu.VMEM((ns * (WIDTH // RW_GROUP), HEAD_DIM, RW_GROUP), F32)],
        compiler_params=_params("parallel", "arbitrary"),
        name="rwkv",
    )(zr3, zr3, zl3, zl3, *consts)


QK_SCALE_LOG2 = HEAD_DIM ** -0.5 * float(np.log2(np.e))
AUX_BLOCKS = LANES // N_HEADS
HALF_HEADS = N_HEADS // 2
VT_ROWS = 80


def _moba_prep_kernel(zm_ref, pos_ref, invf_ref, ropep_ref, qg_ref, kg_ref, hsum_ref,
                      q_out, k_out, v_out, km_ref, *, n_blocks):
    i = pl.program_id(1)
    TB = zm_ref.shape[0]

    @pl.when(i == 0)
    def _():
        km_ref[...] = jnp.zeros_like(km_ref)

    zm = zm_ref[...]
    ang = invf_ref[...] * pos_ref[...].astype(F32)
    cs = jnp.concatenate([jnp.cos(ang), jnp.sin(ang)], axis=0)
    cs_hi = cs.astype(BF16).astype(F32)
    tab = _dot_tn(jnp.concatenate([cs_hi, cs - cs_hi], axis=0), ropep_ref[...])
    lane = lax.broadcasted_iota(jnp.int32, (TB, LANES), 1)
    rep = WIDTH // LANES
    half = ROT_DIM // 2
    cos = jnp.concatenate([tab[:, 0:LANES] + jnp.where(lane % HEAD_DIM >= ROT_DIM, 1.0, 0.0)] * rep, axis=-1)
    sin_lo = jnp.concatenate([tab[:, LANES:2 * LANES]] * rep, axis=-1)
    sin_hi = jnp.concatenate([tab[:, 2 * LANES:3 * LANES]] * rep, axis=-1)
    hsum = hsum_ref[...]

    def norm_rope(t, gain):
        ms = _head_sums(t * t, hsum) * (1.0 / HEAD_DIM)
        t = t * lax.rsqrt(ms + NORM_EPS) * gain
        return (t * cos + pltpu.roll(t, WIDTH - half, axis=1) * sin_lo
                + pltpu.roll(t, half, axis=1) * sin_hi)

    q = norm_rope(zm[:, 0:WIDTH], qg_ref[...])
    k = norm_rope(zm[:, WIDTH:2 * WIDTH], kg_ref[...])
    v = zm[:, 2 * WIDTH:3 * WIDTH]

    km = km_ref[...]
    km_hi, q_hi = km.astype(BF16), q.astype(BF16)
    km_lo, q_lo = (km - km_hi.astype(F32)).astype(BF16), (q - q_hi.astype(F32)).astype(BF16)
    gate2 = _dot_nt(jnp.concatenate([km_hi, km_lo], axis=0), q_hi)
    gate = (gate2[0:LANES] + gate2[LANES:2 * LANES] + _dot_nt(km_hi, q_lo)).reshape(N_HEADS, AUX_BLOCKS, TB)
    n_idx = lax.broadcasted_iota(jnp.int32, gate.shape, 1)
    gsel = jnp.where(n_idx < i, gate, -jnp.inf)
    picked = n_idx < 0
    for _ in range(MOBA_TOPK):
        m = jnp.max(gsel, axis=1, keepdims=True)
        idx = jnp.min(jnp.where(gsel == m, n_idx, AUX_BLOCKS), axis=1, keepdims=True)
        pick = n_idx == idx
        picked = picked | (pick & (m > -jnp.inf))
        gsel = jnp.where(pick, -jnp.inf, gsel)
    keep = picked | (n_idx == i) | (n_idx >= n_blocks)
    bias = jnp.where(keep, 0.0, NEG_BIG).reshape(LANES, TB).T
    bias_by_half = [pltpu.roll(bias, HEAD_DIM, axis=1), bias]

    rowk = lax.broadcasted_iota(jnp.int32, km.shape, 0)
    lanek = lax.broadcasted_iota(jnp.int32, km.shape, 1)
    mine = (rowk % AUX_BLOCKS == i) & (lanek // HEAD_DIM == rowk // AUX_BLOCKS)
    km_ref[...] = jnp.where(mine, jnp.mean(k, axis=0, keepdims=True), km)

    ones_rows = (lax.broadcasted_iota(jnp.int32, (VT_ROWS - HEAD_DIM, TB), 0) == 0).astype(F32)
    for pr in range(N_HEADS // 2):
        ps = slice(pr * LANES, (pr + 1) * LANES)
        qp, kp = q[:, ps] * QK_SCALE_LOG2, k[:, ps]
        vt = v[:, ps].T
        for odd in range(2):
            h = 2 * pr + odd
            is_data = (lane >= HEAD_DIM) if odd else (lane < HEAD_DIM)
            aux_base = 0 if odd else HEAD_DIM
            upper = h // HALF_HEADS
            own_lane = aux_base + (h % HALF_HEADS) * AUX_BLOCKS + i
            q_out[h] = jnp.where(is_data, qp, bias_by_half[1 - upper if odd else upper]).astype(BF16)
            k_out[h] = jnp.where(is_data, kp, jnp.where(lane == own_lane, 1.0, 0.0)).astype(BF16)
            v_out[h] = jnp.concatenate([vt[odd * HEAD_DIM:(odd + 1) * HEAD_DIM], ones_rows], axis=0).astype(BF16)


def _rope_tables():
    half = ROT_DIM // 2
    invf = ROPE_THETA ** (-np.arange(half, dtype=np.float64) / half)
    lane = np.arange(LANES) % HEAD_DIM
    f = np.arange(half)[:, None]
    p_cos = ((lane[None, :] < ROT_DIM) & (lane[None, :] % half == f)).astype(np.float64)
    p_lo = -(lane[None, :] == f).astype(np.float64)
    p_hi = (lane[None, :] == f + half).astype(np.float64)
    z = np.zeros_like(p_cos)
    cos_rows = np.concatenate([p_cos, z, z], axis=1)
    sin_rows = np.concatenate([z, p_lo, p_hi], axis=1)
    expand = np.concatenate([cos_rows, sin_rows, cos_rows, sin_rows], axis=0)
    return (jnp.asarray(np.broadcast_to(invf[:, None], (half, MOBA_BLOCK)), F32), jnp.asarray(expand, F32))


def _moba_prep(zm, positions, qg, kg, hsum, batch, seq):
    TB = MOBA_BLOCK
    nb = seq // TB
    assert nb <= AUX_BLOCKS
    zm3 = zm.reshape(batch, seq, 3 * WIDTH)
    pos4 = positions.reshape(batch, nb, 1, TB)
    invf, expand = _rope_tables()
    aug = pl.BlockSpec((None, N_HEADS, TB, LANES), lambda b, i: (b, 0, i, 0))
    aug_shape = jax.ShapeDtypeStruct((batch, N_HEADS, seq, LANES), BF16)
    return pl.pallas_call(
        functools.partial(_moba_prep_kernel, n_blocks=nb),
        grid=(batch, nb),
        in_specs=[pl.BlockSpec((None, TB, 3 * WIDTH), lambda b, i: (b, i, 0)),
                  pl.BlockSpec((None, None, 1, TB), lambda b, i: (b, i, 0, 0)),
                  _const_spec(invf.shape), _const_spec(expand.shape), _const_spec(qg.shape),
                  _const_spec(kg.shape), _const_spec(hsum.shape)],
        out_specs=[aug, aug, pl.BlockSpec((None, N_HEADS, None, VT_ROWS, TB), lambda b, i: (b, 0, i, 0, 0))],
        out_shape=[aug_shape, aug_shape, jax.ShapeDtypeStruct((batch, N_HEADS, nb, VT_ROWS, TB), BF16)],
        scratch_shapes=[pltpu.VMEM((LANES, WIDTH), F32)],
        compiler_params=_params("parallel", "arbitrary"),
        name="moba_prep",
    )(zm3, pos4, invf, expand, qg, kg, hsum)


def _moba_attn_kernel(q_ref, k_ref, v_ref, o_ref, acc_ref, m_ref, s_ref, p_ref, a_ref, mx_ref):
    i = pl.program_id(2)
    TB = q_ref.shape[1]
    HP = q_ref.shape[0]
    n_steps = i + 1

    def block_of(tau):
        return jnp.where(tau == 0, i, jnp.minimum(tau - 1, jnp.maximum(i - 1, 0)))

    def qk(tau, par, masked=False):
        koff = pl.multiple_of(block_of(tau) * TB, TB)
        for hh in range(HP):
            s = _dot_nt(k_ref[hh, pl.ds(koff, TB), :], q_ref[hh])
            if masked:
                key = lax.broadcasted_iota(jnp.int32, (TB, TB), 0)
                qry = lax.broadcasted_iota(jnp.int32, (TB, TB), 1)
                s = jnp.where(key <= qry, s, -1e30)
            s_ref[par, hh] = s
            mx_ref[par, hh] = jnp.max(s, axis=0, keepdims=True)

    def softmax(par):
        for hh in range(HP):
            m_old = m_ref[hh]
            m_new = jnp.maximum(m_old, mx_ref[par, hh])
            m_ref[hh] = m_new
            a_ref[par, hh] = jnp.exp2(m_old - m_new)
            p_ref[par, hh] = jnp.exp2(s_ref[par, hh] - m_new).astype(BF16)

    def pv(tau, par):
        blk = block_of(tau)
        for hh in range(HP):
            acc_ref[hh] = acc_ref[hh] * a_ref[par, hh] + _dot(v_ref[hh, blk], p_ref[par, hh])

    for hh in range(HP):
        m_ref[hh] = jnp.full((1, TB), -1e30, F32)
        acc_ref[hh] = jnp.zeros((VT_ROWS, TB), F32)
    qk(0, 0, masked=True)
    qk(1, 1)
    softmax(0)

    def slot_pair(pp, carry):
        t = 2 * pp
        qk(t + 2, 0)
        pv(t, 0)
        softmax(1)
        qk(t + 3, 1)
        pv(t + 1, 1)
        softmax(0)
        return carry

    full_pairs = jnp.maximum(n_steps - 2, 0) // 2
    lax.fori_loop(0, full_pairs, slot_pair, 0)
    t0 = 2 * full_pairs
    left = n_steps - t0

    @pl.when(left == 1)
    def _():
        pv(t0, 0)

    @pl.when(left == 2)
    def _():
        pv(t0, 0)
        softmax(1)
        pv(t0 + 1, 1)

    @pl.when(left == 3)
    def _():
        qk(t0 + 2, 0)
        pv(t0, 0)
        softmax(1)
        pv(t0 + 1, 1)
        softmax(0)
        pv(t0 + 2, 0)

    outs = []
    for hh in range(HP):
        acc = acc_ref[hh]
        out_t = jnp.concatenate([acc / acc[HEAD_DIM:HEAD_DIM + 1, :], jnp.zeros((LANES - VT_ROWS, TB), F32)],
                                axis=0)
        outs.append(out_t.T)
    lane = lax.broadcasted_iota(jnp.int32, (TB, LANES), 1)
    for pr in range(HP // 2):
        o_ref[:, pr * LANES:(pr + 1) * LANES] = jnp.where(
            lane < HEAD_DIM, outs[2 * pr], pltpu.roll(outs[2 * pr + 1], HEAD_DIM, axis=1)).astype(o_ref.dtype)


ATTN_HEADS = 8


def _moba_attn(qa, ka, va, batch, seq):
    TB = MOBA_BLOCK
    nb = seq // TB
    HP = ATTN_HEADS
    return pl.pallas_call(
        _moba_attn_kernel,
        grid=(batch, N_HEADS // HP, nb),
        in_specs=[pl.BlockSpec((None, HP, TB, LANES), lambda b, p, i: (b, p, i, 0)),
                  pl.BlockSpec((None, HP, seq, LANES), lambda b, p, i: (b, p, 0, 0)),
                  pl.BlockSpec((None, HP, nb, VT_ROWS, TB), lambda b, p, i: (b, p, 0, 0, 0))],
        out_specs=pl.BlockSpec((None, TB, HP * HEAD_DIM), lambda b, p, i: (b, i, p)),
        out_shape=jax.ShapeDtypeStruct((batch, seq, WIDTH), BF16),
        scratch_shapes=[pltpu.VMEM((HP, VT_ROWS, TB), F32), pltpu.VMEM((HP, 1, TB), F32),
                        pltpu.VMEM((2, HP, TB, TB), F32), pltpu.VMEM((2, HP, TB, TB), BF16),
                        pltpu.VMEM((2, HP, 1, TB), F32), pltpu.VMEM((2, HP, 1, TB), F32)],
        compiler_params=_params("parallel", "parallel", "arbitrary"),
        name="moba_attn",
    )(qa, ka, va)


def _merge_kernel(x_ref, ya_ref, yb_ref, gate_ref, wa_ref, wb_ref, wo_ref, g2_ref, x1_ref, h2_ref):
    half = x_ref.shape[0] // 2
    halves = (slice(0, half), slice(half, 2 * half))
    ua = [_dot(ya_ref[rows, :], wa_ref[...]) for rows in halves]
    ub = [_dot(yb_ref[rows, :], wb_ref[...]) for rows in halves]
    for hi, rows in enumerate(halves):
        gate = gate_ref[rows, :].astype(F32)
        mix = (gate[:, 0:D_MODEL] * ua[hi] + gate[:, D_MODEL:] * ub[hi]).astype(BF16)
        x1 = x_ref[rows, :] + _dot(mix, wo_ref[...])
        x1_ref[rows, :] = x1
        ms = jnp.mean(x1 * x1, axis=-1, keepdims=True)
        h2_ref[rows, :] = (x1 * lax.rsqrt(ms + NORM_EPS) * g2_ref[...]).astype(BF16)


def _merge(x2, ya, yb, gates, wa, wb, wo, g2, tm):
    t = x2.shape[0]
    row = lambda n: pl.BlockSpec((tm, n), lambda i: (i, 0))
    return pl.pallas_call(
        _merge_kernel,
        grid=(t // tm,),
        in_specs=[row(D_MODEL), row(WIDTH), row(WIDTH), row(2 * D_MODEL), _const_spec(wa.shape),
                  _const_spec(wb.shape), _const_spec(wo.shape), _const_spec(g2.shape)],
        out_specs=[row(D_MODEL), row(D_MODEL)],
        out_shape=[jax.ShapeDtypeStruct((t, D_MODEL), F32), jax.ShapeDtypeStruct((t, D_MODEL), BF16)],
        compiler_params=_params("parallel"),
        name="merge",
    )(x2, ya, yb, gates, wa, wb, wo, g2)


FFN_HALO = 16
FFN_COLS = (768, 768, 768, 512)


def _ffn_kernel(h_ref, hp_ref, x1_ref, wu_ref, cw_ref, cb_ref, wd_ref, o_ref, *, tiles_per_seq):
    i = pl.program_id(0)
    tm = h_ref.shape[0]
    h = h_ref[...]
    halo = jnp.where(i % tiles_per_seq == 0, jnp.zeros_like(hp_ref[...]), hp_ref[...])
    h_ext = jnp.concatenate([halo, h], axis=0)
    starts = np.cumsum((0,) + FFN_COLS)
    groups = [slice(int(starts[g]), int(starts[g + 1])) for g in range(len(FFN_COLS))]

    def up(cs):
        gate_cs = slice(D_FF + cs.start, D_FF + cs.stop)
        return _dot(h_ext, wu_ref[:, cs]), _dot(h, wu_ref[:, gate_cs])

    acc = x1_ref[...]
    nxt = up(groups[0])
    for g, cs in enumerate(groups):
        a, b = nxt
        if g + 1 < len(groups):
            nxt = up(groups[g + 1])
        conv = (a[FFN_HALO:, :] * cw_ref[2:3, cs] + a[FFN_HALO - 1:FFN_HALO - 1 + tm, :] * cw_ref[1:2, cs]
                + a[FFN_HALO - 2:FFN_HALO - 2 + tm, :] * cw_ref[0:1, cs] + cb_ref[:, cs])
        gelu = 0.5 * conv * (1.0 + lax.erf(conv * (2.0 ** -0.5)))
        acc = acc + _dot((gelu * b).astype(BF16), wd_ref[cs, :])
    o_ref[...] = acc


def _ffn(h2, x1, wu, cw, cb, wd, tm, seq):
    t = h2.shape[0]
    row = lambda n: pl.BlockSpec((tm, n), lambda i: (i, 0))
    halo = pl.BlockSpec((FFN_HALO, D_MODEL), lambda i: (jnp.maximum(i * (tm // FFN_HALO) - 1, 0), 0))
    return pl.pallas_call(
        functools.partial(_ffn_kernel, tiles_per_seq=seq // tm),
        grid=(t // tm,),
        in_specs=[row(D_MODEL), halo, row(D_MODEL), _const_spec(wu.shape),
                  _const_spec(cw.shape), _const_spec(cb.shape), _const_spec(wd.shape)],
        out_specs=row(D_MODEL),
        out_shape=jax.ShapeDtypeStruct((t, D_MODEL), F32),
        compiler_params=_params("parallel"),
        name="ffn",
    )(h2, h2, x1, wu, cw, cb, wd)


def _pad_lora_cols(a):
    z = lambda n: jnp.zeros(a.shape[:-1] + (n,), a.dtype)
    o1, o2 = DECAY_LORA, DECAY_LORA + AAA_LORA
    return jnp.concatenate([a[..., :o1], z(LANES - DECAY_LORA), a[..., o1:o2], z(LANES - AAA_LORA),
                            a[..., o2:], z(2 * LANES - GATE_LORA)], axis=-1)


def _pad_rows(a, n):
    return jnp.concatenate([a, jnp.zeros((n - a.shape[0],) + a.shape[1:], a.dtype)], axis=0)


def kernel(x, positions, norm1_g, w_in, rwkv_mu, w_decay_up, decay_bias, w_aaa_up, aaa_bias, w_gate_up, rwkv_k_k, rwkv_k_a, rwkv_r_k, rwkv_ln_g, rwkv_ln_b, q_norm_g, k_norm_g, w_branch_a, w_branch_b, w_out, norm2_g, w_ffn_up, ffn_conv_w, ffn_conv_b, w_ffn_down):
    batch, seq, _ = x.shape
    depth = norm1_g.shape[0]
    assert seq % MOBA_BLOCK == 0 and seq % 512 == 0
    t = batch * seq
    row = lambda a: a.reshape(1, -1)
    c3 = 3 * WIDTH
    rwkv_in = c3 + DECAY_LORA + AAA_LORA + GATE_LORA
    hsum = _rwkv_masks()[1]

    x2 = x.reshape(t, D_MODEL)
    for l in range(depth):
        wi = w_in[l]
        wr = wi[:, :c3].astype(BF16)
        wl = _pad_lora_cols(wi[:, c3:rwkv_in]).astype(BF16)
        wm = wi[:, rwkv_in:rwkv_in + c3].astype(BF16)
        wg = wi[:, rwkv_in + c3:].astype(BF16)
        zr, zl, zm, gates = _inproj(x2, row(norm1_g[l]), wr, wl, wm, wg, tm=512)

        consts = [row(rwkv_mu[l][:c3]), _pad_lora_cols(row(rwkv_mu[l][c3:])),
                  _pad_rows(w_decay_up[l], LANES).astype(BF16), row(decay_bias[l]),
                  _pad_rows(w_aaa_up[l], LANES).astype(BF16), row(aaa_bias[l]),
                  _pad_rows(w_gate_up[l], 2 * LANES).astype(BF16),
                  row(rwkv_k_k[l]), row(rwkv_k_a[l]), row(rwkv_r_k[l]),
                  row(rwkv_ln_g[l]), row(rwkv_ln_b[l])]
        ya = _rwkv(zr, zl, consts, batch, seq).reshape(t, WIDTH)

        tile8 = lambda a: row(jnp.tile(a, N_HEADS))
        qa, ka, va = _moba_prep(zm, positions, tile8(q_norm_g[l]), tile8(k_norm_g[l]), hsum, batch, seq)
        yb = _moba_attn(qa, ka, va, batch, seq).reshape(t, WIDTH)

        x1, h2 = _merge(x2, ya, yb, gates, w_branch_a[l].astype(BF16), w_branch_b[l].astype(BF16),
                        w_out[l].astype(BF16), row(norm2_g[l]), tm=512)

        x2 = _ffn(h2, x1, w_ffn_up[l].astype(BF16), ffn_conv_w[l],
                  row(ffn_conv_b[l]), w_ffn_down[l].astype(BF16), tm=512, seq=seq)
    return x2.reshape(batch, seq, D_MODEL)
```

```python
import functools

import numpy as np
import jax
import jax.numpy as jnp
from jax import lax
from jax.experimental import pallas as pl
from jax.experimental.pallas import tpu as pltpu

F32 = jnp.float32
BF16 = jnp.bfloat16

D_MODEL = 1024
HEAD_DIM = 64
N_HEADS = 8
WIDTH = N_HEADS * HEAD_DIM
DECAY_LORA = 64
AAA_LORA = 64
GATE_LORA = 160
LORA_PAD = 512
MOBA_BLOCK = 256
MOBA_TOPK = 3
ROT_DIM = HEAD_DIM // 4
ROPE_THETA = 500000.0
D_FF = 2816
NORM_EPS = 1e-6
GN_EPS = 64e-5
LANES = 128
CHUNK = 64
NEG_BIG = -(2.0 ** 100)
VMEM_LIMIT = 56 * 1024 * 1024


def _sigmoid(x):
    return 1.0 / (1.0 + jnp.exp(-x))


def _dot(a, b, precision=None):
    return jnp.dot(a, b, preferred_element_type=F32, precision=precision)


def _dot_nt(a, b, precision=None):
    return lax.dot_general(a, b, (((1,), (1,)), ((), ())), preferred_element_type=F32,
                           precision=precision)


def _head_sums(x, bd_ones):
    g = bd_ones.shape[0]
    return jnp.concatenate([_dot(x[:, j:j + g].astype(BF16), bd_ones) for j in range(0, x.shape[1], g)],
                           axis=1)


def _dot_tn(a, b, precision=None):
    return lax.dot_general(a, b, (((0,), (0,)), ((), ())), preferred_element_type=F32,
                           precision=precision)


def _params(*sem):
    return pltpu.CompilerParams(dimension_semantics=sem, vmem_limit_bytes=VMEM_LIMIT)


def _const_spec(shape):
    nd = len(shape)
    return pl.BlockSpec(shape, lambda *_: (0,) * nd, pipeline_mode=pl.Buffered(1))


def _inproj_kernel(x_ref, g_ref, wr_ref, wl_ref, wm_ref, wg_ref,
                   zr_ref, zl_ref, zm_ref, gate_ref):
    half = x_ref.shape[0] // 2
    for rows in (slice(0, half), slice(half, 2 * half)):
        x = x_ref[rows, :]
        ms = jnp.mean(x * x, axis=-1, keepdims=True)
        h = (x * lax.rsqrt(ms + NORM_EPS) * g_ref[...]).astype(BF16)
        zr_ref[rows, :] = _dot(h, wr_ref[...])
        zl_ref[rows, :] = _dot(h, wl_ref[...])
        zm_ref[rows, :] = _dot(h, wm_ref[...])
        gate_ref[rows, :] = _sigmoid(_dot(h, wg_ref[...])).astype(BF16)


def _inproj(x2, g, wr, wl, wm, wg, tm):
    t = x2.shape[0]
    row = lambda n: pl.BlockSpec((tm, n), lambda i: (i, 0))
    return pl.pallas_call(
        _inproj_kernel,
        grid=(t // tm,),
        in_specs=[row(D_MODEL), _const_spec(g.shape), _const_spec(wr.shape), _const_spec(wl.shape),
                  _const_spec(wm.shape), _const_spec(wg.shape)],
        out_specs=[row(3 * WIDTH), row(LORA_PAD), row(3 * WIDTH), row(2 * D_MODEL)],
        out_shape=[jax.ShapeDtypeStruct((t, 3 * WIDTH), F32),
                   jax.ShapeDtypeStruct((t, LORA_PAD), F32),
                   jax.ShapeDtypeStruct((t, 3 * WIDTH), F32),
                   jax.ShapeDtypeStruct((t, 2 * D_MODEL), BF16)],
        compiler_params=_params("parallel"),
        name="inproj",
    )(x2, g, wr, wl, wm, wg)


RW_ROWS = 256
RW_SEQS = 4
RW_GROUP = 256


def _rwkv_kernel(zr_ref, zrp_ref, zl_ref, zlp_ref, mur_ref, mul_ref, wd_ref, db_ref, wa_ref,
                 ab_ref, wg_ref, kk_ref, ka_ref, rk_ref, lng_ref, lnb_ref, cum_ref, bd_ref,
                 o_ref, state_ref):
    c = pl.program_id(1)
    n_seq, TS = zr_ref.shape[0], zr_ref.shape[1]
    TL = n_seq * TS
    L = CHUNK
    G = RW_GROUP

    @pl.when(c == 0)
    def _():
        state_ref[...] = jnp.zeros_like(state_ref)

    def token_shift(z_ref, zp_ref, mu_ref):
        z = z_ref[...].reshape(TL, z_ref.shape[2])
        row = lax.broadcasted_iota(jnp.int32, z.shape, 0)
        zs = pltpu.roll(z, 1, axis=0)
        for s in range(n_seq):
            zs = jnp.where(row == s * TS, jnp.where(c == 0, 0.0, zp_ref[s, 7:8, :]), zs)
        return z + mu_ref[...] * (zs - z)

    zr = token_shift(zr_ref, zrp_ref, mur_ref)
    zl = token_shift(zl_ref, zlp_ref, mul_ref)
    r = zr[:, 0:WIDTH]
    k = zr[:, WIDTH:2 * WIDTH]
    v = zr[:, 2 * WIDTH:3 * WIDTH]
    xw = zl[:, 0:LANES]
    xa = zl[:, LANES:2 * LANES]
    xg = zl[:, 2 * LANES:4 * LANES]

    dd = db_ref[...] + _dot(jnp.tanh(xw).astype(BF16), wd_ref[...])
    w_log = -(jnp.maximum(-dd, 0.0) + jnp.log(1.0 + jnp.exp(-jnp.abs(dd)))) - 0.5
    lw = -jnp.exp(w_log)
    asig = _sigmoid(ab_ref[...] + _dot(xa.astype(BF16), wa_ref[...]))
    g = _dot(_sigmoid(xg).astype(BF16), wg_ref[...])
    bdm = bd_ref[...]
    kkf = k * kk_ref[...]
    kk = kkf * lax.rsqrt(jnp.maximum(_head_sums(kkf * kkf, bdm), 1e-24))
    kmod = k * (1.0 + (asig - 1.0) * ka_ref[...])
    bonus = _head_sums(r * kmod * rk_ref[...], bdm) * v
    b = kk * asig

    lw_hi = lw.astype(BF16)
    lw_lo = (lw - lw_hi.astype(F32)).astype(BF16)
    cum = cum_ref[...]
    cums = [_dot(cum, lw_hi[ci * L:(ci + 1) * L]) + _dot(cum, lw_lo[ci * L:(ci + 1) * L]) for ci in range(TL // L)]
    cw = jnp.concatenate([cs[0:L] for cs in cums], axis=0)
    cw_end = jnp.concatenate([cs[L:2 * L] for cs in cums], axis=0)
    e_neg = jnp.exp(-cw)
    e_end = jnp.exp(cw_end - cw)
    a_til = (-kk * jnp.exp(cw - lw)).astype(BF16)
    r_til = (r * jnp.exp(cw)).astype(BF16)
    b_til = (b * e_neg).astype(BF16)
    k_til = (kmod * e_neg).astype(BF16)
    b_hat = b * e_end
    k_hat = kmod * e_end
    w_end = jnp.exp(cw_end)
    vb = v.astype(BF16)

    def bd(x):
        return jnp.concatenate([x.astype(BF16)] * (G // L), axis=0) * bdm

    lane_lo = lax.broadcasted_iota(jnp.int32, (L, LANES), 1) < HEAD_DIM

    def head_t(x):
        t = jnp.concatenate([x, x], axis=0).T
        return jnp.concatenate([jnp.where(lane_lo, t[2 * j * L:(2 * j + 1) * L], t[(2 * j + 1) * L:(2 * j + 2) * L])
                                for j in range(G // LANES)], axis=1)

    row2 = lax.broadcasted_iota(jnp.int32, (2 * L, G), 0)
    lane_t = lax.broadcasted_iota(jnp.int32, (2 * L, G), 1) % L
    tri2 = lane_t < jnp.where(row2 < L, row2, row2 - L + 1)
    eye = (lax.broadcasted_iota(jnp.int32, (L, G), 1) % L
           == lax.broadcasted_iota(jnp.int32, (L, G), 0)).astype(F32)

    n_chunks, n_groups = TL // L, WIDTH // G
    pairs = [(ci, gi) for ci in range(n_chunks) for gi in range(n_groups)]

    def blk(x, cg):
        ci, gi = cg
        return x[ci * L:(ci + 1) * L, gi * G:(gi + 1) * G]

    p = {cg: jnp.concatenate([blk(a_til, cg), blk(r_til, cg)], axis=0) for cg in pairs}
    ab_rb = {cg: jnp.where(tri2, _dot_nt(p[cg], bd(blk(b_til, cg))), 0.0) for cg in pairs}
    ak_rk = {cg: jnp.where(tri2, _dot_nt(p[cg], bd(blk(k_til, cg))), 0.0) for cg in pairs}
    tinv = {cg: eye + ab_rb[cg][0:L] for cg in pairs}
    pw = {cg: _dot(ab_rb[cg][0:L].astype(BF16), bd(ab_rb[cg][0:L])) for cg in pairs}
    for _ in range(int(np.log2(L)) - 2):
        both = {cg: _dot(jnp.concatenate([pw[cg], tinv[cg]], axis=0).astype(BF16), bd(pw[cg])) for cg in pairs}
        pw = {cg: both[cg][0:L] for cg in pairs}
        tinv = {cg: tinv[cg] + both[cg][L:2 * L] for cg in pairs}
    tinv = {cg: (tinv[cg] + _dot(tinv[cg].astype(BF16), bd(pw[cg]))).astype(BF16) for cg in pairs}
    bd_v = {cg: bd(blk(vb, cg)) for cg in pairs}
    kv = {cg: _dot(ak_rk[cg].astype(BF16), bd_v[cg]) for cg in pairs}
    kt_v = {cg: _dot(head_t(blk(k_hat, cg)).astype(BF16), bd_v[cg]) for cg in pairs}
    arb_bt = {cg: jnp.concatenate([ab_rb[cg][L:2 * L], head_t(blk(b_hat, cg))], axis=0).astype(BF16)
              for cg in pairs}
    w_row = {cg: head_t(blk(w_end, cg)) for cg in pairs}

    chains = [(s, gi) for s in range(n_seq) for gi in range(n_groups)]
    seq_chunks = TS // L
    st = {sg: state_ref[sg[0] * n_groups + sg[1]] for sg in chains}
    y_blk = {}
    for ci in range(seq_chunks):
        cg = {sg: (sg[0] * seq_chunks + ci, sg[1]) for sg in chains}
        ps = {sg: _dot(p[cg[sg]], bd(st[sg])) for sg in chains}
        u = {sg: _dot(tinv[cg[sg]], bd(ps[sg][0:L] + kv[cg[sg]][0:L])) for sg in chains}
        yu = {sg: _dot(arb_bt[cg[sg]], bd(u[sg])) for sg in chains}
        st = {sg: w_row[cg[sg]] * st[sg] + yu[sg][L:2 * L] + kt_v[cg[sg]] for sg in chains}
        for sg in chains:
            y_blk[cg[sg]] = ps[sg][L:2 * L] + kv[cg[sg]][L:2 * L] + yu[sg][0:L]
    for sg in chains:
        state_ref[sg[0] * n_groups + sg[1]] = st[sg]
    y = jnp.concatenate([jnp.concatenate([y_blk[(ci, gi)] for gi in range(n_groups)], axis=1)
                         for ci in range(n_chunks)], axis=0)

    inv_n = 1.0 / HEAD_DIM
    mu = _head_sums(y, bdm) * inv_n
    yc = y - mu
    var = _head_sums(yc * yc, bdm) * inv_n
    yn = yc * lax.rsqrt(var + GN_EPS) * lng_ref[...] + lnb_ref[...]
    o_ref[...] = ((yn + bonus) * g).astype(o_ref.dtype).reshape(o_ref.shape)


def _rwkv_masks():
    t = np.arange(CHUNK)
    tri = t[:, None] >= t[None, :]
    cum = np.concatenate([tri, np.ones_like(tri)], axis=0)
    g = np.arange(RW_GROUP)
    bd = (g[:, None] // HEAD_DIM) == (g[None, :] // HEAD_DIM)
    return [jnp.asarray(m, BF16) for m in (cum, bd)]


def _rwkv(zr, zl, consts, batch, seq):
    TL = RW_ROWS
    nt = seq // TL
    ns = RW_SEQS if batch % RW_SEQS == 0 else 1
    cur = lambda n: pl.BlockSpec((ns, TL, n), lambda b, c: (b, c, 0))
    prv = lambda n: pl.BlockSpec((ns, 8, n), lambda b, c: (b, jnp.maximum(c * (TL // 8) - 1, 0), 0))
    zr3 = zr.reshape(batch, seq, 3 * WIDTH)
    zl3 = zl.reshape(batch, seq, LORA_PAD)
    consts = list(consts) + _rwkv_masks()
    return pl.pallas_call(
        _rwkv_kernel,
        grid=(batch // ns, nt),
        in_specs=[cur(3 * WIDTH), prv(3 * WIDTH), cur(LORA_PAD), prv(LORA_PAD)]
                 + [_const_spec(a.shape) for a in consts],
        out_specs=pl.BlockSpec((ns, TL, WIDTH), lambda b, c: (b, c, 0)),
        out_shape=jax.ShapeDtypeStruct((batch, seq, WIDTH), BF16),
        scratch_shapes=[pltpu.VMEM((ns * (WIDTH // RW_GROUP), HEAD_DIM, RW_GROUP), F32)],
        compiler_params=_params("parallel", "arbitrary"),
        name="rwkv",
    )(zr3, zr3, zl3, zl3, *consts)


QK_SCALE_LOG2 = HEAD_DIM ** -0.5 * float(np.log2(np.e))
AUX_BLOCKS = LANES // N_HEADS
HALF_HEADS = N_HEADS // 2
VT_ROWS = 80


def _moba_prep_kernel(zm_ref, pos_ref, invf_ref, ropep_ref, qg_ref, kg_ref, hsum_ref,
                      q_out, k_out, v_out, km_ref, *, n_blocks):
    i = pl.program_id(1)
    TB = zm_ref.shape[0]

    @pl.when(i == 0)
    def _():
        km_ref[...] = jnp.zeros_like(km_ref)

    zm = zm_ref[...]
    ang = invf_ref[...] * pos_ref[...].astype(F32)
    cs = jnp.concatenate([jnp.cos(ang), jnp.sin(ang)], axis=0)
    cs_hi = cs.astype(BF16).astype(F32)
    tab = _dot_tn(jnp.concatenate([cs_hi, cs - cs_hi], axis=0), ropep_ref[...])
    lane = lax.broadcasted_iota(jnp.int32, (TB, LANES), 1)
    rep = WIDTH // LANES
    half = ROT_DIM // 2
    cos = jnp.concatenate([tab[:, 0:LANES] + jnp.where(lane % HEAD_DIM >= ROT_DIM, 1.0, 0.0)] * rep, axis=-1)
    sin_lo = jnp.concatenate([tab[:, LANES:2 * LANES]] * rep, axis=-1)
    sin_hi = jnp.concatenate([tab[:, 2 * LANES:3 * LANES]] * rep, axis=-1)
    hsum = hsum_ref[...]

    def norm_rope(t, gain):
        ms = _head_sums(t * t, hsum) * (1.0 / HEAD_DIM)
        t = t * lax.rsqrt(ms + NORM_EPS) * gain
        return (t * cos + pltpu.roll(t, WIDTH - half, axis=1) * sin_lo
                + pltpu.roll(t, half, axis=1) * sin_hi)

    q = norm_rope(zm[:, 0:WIDTH], qg_ref[...])
    k = norm_rope(zm[:, WIDTH:2 * WIDTH], kg_ref[...])
    v = zm[:, 2 * WIDTH:3 * WIDTH]

    km = km_ref[...]
    km_hi, q_hi = km.astype(BF16), q.astype(BF16)
    km_lo, q_lo = (km - km_hi.astype(F32)).astype(BF16), (q - q_hi.astype(F32)).astype(BF16)
    gate2 = _dot_nt(jnp.concatenate([km_hi, km_lo], axis=0), q_hi)
    gate = (gate2[0:LANES] + gate2[LANES:2 * LANES] + _dot_nt(km_hi, q_lo)).reshape(N_HEADS, AUX_BLOCKS, TB)
    n_idx = lax.broadcasted_iota(jnp.int32, gate.shape, 1)
    gsel = jnp.where(n_idx < i, gate, -jnp.inf)
    picked = n_idx < 0
    for _ in range(MOBA_TOPK):
        m = jnp.max(gsel, axis=1, keepdims=True)
        idx = jnp.min(jnp.where(gsel == m, n_idx, AUX_BLOCKS), axis=1, keepdims=True)
        pick = n_idx == idx
        picked = picked | (pick & (m > -jnp.inf))
        gsel = jnp.where(pick, -jnp.inf, gsel)
    keep = picked | (n_idx == i) | (n_idx >= n_blocks)
    bias = jnp.where(keep, 0.0, NEG_BIG).reshape(LANES, TB).T
    bias_by_half = [pltpu.roll(bias, HEAD_DIM, axis=1), bias]

    rowk = lax.broadcasted_iota(jnp.int32, km.shape, 0)
    lanek = lax.broadcasted_iota(jnp.int32, km.shape, 1)
    mine = (rowk % AUX_BLOCKS == i) & (lanek // HEAD_DIM == rowk // AUX_BLOCKS)
    km_ref[...] = jnp.where(mine, jnp.mean(k, axis=0, keepdims=True), km)

    ones_rows = (lax.broadcasted_iota(jnp.int32, (VT_ROWS - HEAD_DIM, TB), 0) == 0).astype(F32)
    for pr in range(N_HEADS // 2):
        ps = slice(pr * LANES, (pr + 1) * LANES)
        qp, kp = q[:, ps] * QK_SCALE_LOG2, k[:, ps]
        vt = v[:, ps].T
        for odd in range(2):
            h = 2 * pr + odd
            is_data = (lane >= HEAD_DIM) if odd else (lane < HEAD_DIM)
            aux_base = 0 if odd else HEAD_DIM
            upper = h // HALF_HEADS
            own_lane = aux_base + (h % HALF_HEADS) * AUX_BLOCKS + i
            q_out[h] = jnp.where(is_data, qp, bias_by_half[1 - upper if odd else upper]).astype(BF16)
            k_out[h] = jnp.where(is_data, kp, jnp.where(lane == own_lane, 1.0, 0.0)).astype(BF16)
            v_out[h] = jnp.concatenate([vt[odd * HEAD_DIM:(odd + 1) * HEAD_DIM], ones_rows], axis=0).astype(BF16)


def _rope_tables():
    half = ROT_DIM // 2
    invf = ROPE_THETA ** (-np.arange(half, dtype=np.float64) / half)
    lane = np.arange(LANES) % HEAD_DIM
    f = np.arange(half)[:, None]
    p_cos = ((lane[None, :] < ROT_DIM) & (lane[None, :] % half == f)).astype(np.float64)
    p_lo = -(lane[None, :] == f).astype(np.float64)
    p_hi = (lane[None, :] == f + half).astype(np.float64)
    z = np.zeros_like(p_cos)
    cos_rows = np.concatenate([p_cos, z, z], axis=1)
    sin_rows = np.concatenate([z, p_lo, p_hi], axis=1)
    expand = np.concatenate([cos_rows, sin_rows, cos_rows, sin_rows], axis=0)
    return (jnp.asarray(np.broadcast_to(invf[:, None], (half, MOBA_BLOCK)), F32), jnp.asarray(expand, F32))


def _moba_prep(zm, positions, qg, kg, hsum, batch, seq):
    TB = MOBA_BLOCK
    nb = seq // TB
    assert nb <= AUX_BLOCKS
    zm3 = zm.reshape(batch, seq, 3 * WIDTH)
    pos4 = positions.reshape(batch, nb, 1, TB)
    invf, expand = _rope_tables()
    aug = pl.BlockSpec((None, N_HEADS, TB, LANES), lambda b, i: (b, 0, i, 0))
    aug_shape = jax.ShapeDtypeStruct((batch, N_HEADS, seq, LANES), BF16)
    return pl.pallas_call(
        functools.partial(_moba_prep_kernel, n_blocks=nb),
        grid=(batch, nb),
        in_specs=[pl.BlockSpec((None, TB, 3 * WIDTH), lambda b, i: (b, i, 0)),
                  pl.BlockSpec((None, None, 1, TB), lambda b, i: (b, i, 0, 0)),
                  _const_spec(invf.shape), _const_spec(expand.shape), _const_spec(qg.shape),
                  _const_spec(kg.shape), _const_spec(hsum.shape)],
        out_specs=[aug, aug, pl.BlockSpec((None, N_HEADS, None, VT_ROWS, TB), lambda b, i: (b, 0, i, 0, 0))],
        out_shape=[aug_shape, aug_shape, jax.ShapeDtypeStruct((batch, N_HEADS, nb, VT_ROWS, TB), BF16)],
        scratch_shapes=[pltpu.VMEM((LANES, WIDTH), F32)],
        compiler_params=_params("parallel", "arbitrary"),
        name="moba_prep",
    )(zm3, pos4, invf, expand, qg, kg, hsum)


def _moba_attn_kernel(q_ref, k_ref, v_ref, o_ref, acc_ref, m_ref, s_ref, p_ref, a_ref, mx_ref):
    i = pl.program_id(2)
    TB = q_ref.shape[1]
    HP = q_ref.shape[0]
    n_steps = i + 1

    def block_of(tau):
        return jnp.where(tau == 0, i, jnp.minimum(tau - 1, jnp.maximum(i - 1, 0)))

    def qk(tau, par, masked=False):
        koff = pl.multiple_of(block_of(tau) * TB, TB)
        for hh in range(HP):
            s = _dot_nt(k_ref[hh, pl.ds(koff, TB), :], q_ref[hh])
            if masked:
                key = lax.broadcasted_iota(jnp.int32, (TB, TB), 0)
                qry = lax.broadcasted_iota(jnp.int32, (TB, TB), 1)
                s = jnp.where(key <= qry, s, -1e30)
            s_ref[par, hh] = s
            mx_ref[par, hh] = jnp.max(s, axis=0, keepdims=True)

    def softmax(par):
        for hh in range(HP):
            m_old = m_ref[hh]
            m_new = jnp.maximum(m_old, mx_ref[par, hh])
            m_ref[hh] = m_new
            a_ref[par, hh] = jnp.exp2(m_old - m_new)
            p_ref[par, hh] = jnp.exp2(s_ref[par, hh] - m_new).astype(BF16)

    def pv(tau, par):
        blk = block_of(tau)
        for hh in range(HP):
            acc_ref[hh] = acc_ref[hh] * a_ref[par, hh] + _dot(v_ref[hh, blk], p_ref[par, hh])

    for hh in range(HP):
        m_ref[hh] = jnp.full((1, TB), -1e30, F32)
        acc_ref[hh] = jnp.zeros((VT_ROWS, TB), F32)
    qk(0, 0, masked=True)
    qk(1, 1)
    softmax(0)

    def slot_pair(pp, carry):
        t = 2 * pp
        qk(t + 2, 0)
        pv(t, 0)
        softmax(1)
        qk(t + 3, 1)
        pv(t + 1, 1)
        softmax(0)
        return carry

    full_pairs = jnp.maximum(n_steps - 2, 0) // 2
    lax.fori_loop(0, full_pairs, slot_pair, 0)
    t0 = 2 * full_pairs
    left = n_steps - t0

    @pl.when(left == 1)
    def _():
        pv(t0, 0)

    @pl.when(left == 2)
    def _():
        pv(t0, 0)
        softmax(1)
        pv(t0 + 1, 1)

    @pl.when(left == 3)
    def _():
        qk(t0 + 2, 0)
        pv(t0, 0)
        softmax(1)
        pv(t0 + 1, 1)
        softmax(0)
        pv(t0 + 2, 0)

    outs = []
    for hh in range(HP):
        acc = acc_ref[hh]
        out_t = jnp.concatenate([acc / acc[HEAD_DIM:HEAD_DIM + 1, :], jnp.zeros((LANES - VT_ROWS, TB), F32)],
                                axis=0)
        outs.append(out_t.T)
    lane = lax.broadcasted_iota(jnp.int32, (TB, LANES), 1)
    for pr in range(HP // 2):
        o_ref[:, pr * LANES:(pr + 1) * LANES] = jnp.where(
            lane < HEAD_DIM, outs[2 * pr], pltpu.roll(outs[2 * pr + 1], HEAD_DIM, axis=1)).astype(o_ref.dtype)


ATTN_HEADS = 8


def _moba_attn(qa, ka, va, batch, seq):
    TB = MOBA_BLOCK
    nb = seq // TB
    HP = ATTN_HEADS
    return pl.pallas_call(
        _moba_attn_kernel,
        grid=(batch, N_HEADS // HP, nb),
        in_specs=[pl.BlockSpec((None, HP, TB, LANES), lambda b, p, i: (b, p, i, 0)),
                  pl.BlockSpec((None, HP, seq, LANES), lambda b, p, i: (b, p, 0, 0)),
                  pl.BlockSpec((None, HP, nb, VT_ROWS, TB), lambda b, p, i: (b, p, 0, 0, 0))],
        out_specs=pl.BlockSpec((None, TB, HP * HEAD_DIM), lambda b, p, i: (b, i, p)),
        out_shape=jax.ShapeDtypeStruct((batch, seq, WIDTH), BF16),
        scratch_shapes=[pltpu.VMEM((HP, VT_ROWS, TB), F32), pltpu.VMEM((HP, 1, TB), F32),
                        pltpu.VMEM((2, HP, TB, TB), F32), pltpu.VMEM((2, HP, TB, TB), BF16),
                        pltpu.VMEM((2, HP, 1, TB), F32), pltpu.VMEM((2, HP, 1, TB), F32)],
        compiler_params=_params("parallel", "parallel", "arbitrary"),
        name="moba_attn",
    )(qa, ka, va)


def _merge_kernel(x_ref, ya_ref, yb_ref, gate_ref, wa_ref, wb_ref, wo_ref, g2_ref, x1_ref, h2_ref):
    half = x_ref.shape[0] // 2
    halves = (slice(0, half), slice(half, 2 * half))
    ua = [_dot(ya_ref[rows, :], wa_ref[...]) for rows in halves]
    ub = [_dot(yb_ref[rows, :], wb_ref[...]) for rows in halves]
    for hi, rows in enumerate(halves):
        gate = gate_ref[rows, :].astype(F32)
        mix = (gate[:, 0:D_MODEL] * ua[hi] + gate[:, D_MODEL:] * ub[hi]).astype(BF16)
        x1 = x_ref[rows, :] + _dot(mix, wo_ref[...])
        x1_ref[rows, :] = x1
        ms = jnp.mean(x1 * x1, axis=-1, keepdims=True)
        h2_ref[rows, :] = (x1 * lax.rsqrt(ms + NORM_EPS) * g2_ref[...]).astype(BF16)


def _merge(x2, ya, yb, gates, wa, wb, wo, g2, tm):
    t = x2.shape[0]
    row = lambda n: pl.BlockSpec((tm, n), lambda i: (i, 0))
    return pl.pallas_call(
        _merge_kernel,
        grid=(t // tm,),
        in_specs=[row(D_MODEL), row(WIDTH), row(WIDTH), row(2 * D_MODEL), _const_spec(wa.shape),
                  _const_spec(wb.shape), _const_spec(wo.shape), _const_spec(g2.shape)],
        out_specs=[row(D_MODEL), row(D_MODEL)],
        out_shape=[jax.ShapeDtypeStruct((t, D_MODEL), F32), jax.ShapeDtypeStruct((t, D_MODEL), BF16)],
        compiler_params=_params("parallel"),
        name="merge",
    )(x2, ya, yb, gates, wa, wb, wo, g2)


FFN_HALO = 16
FFN_COLS = (768, 768, 768, 512)


def _ffn_kernel(h_ref, hp_ref, x1_ref, wu_ref, cw_ref, cb_ref, wd_ref, o_ref, *, tiles_per_seq):
    i = pl.program_id(0)
    tm = h_ref.shape[0]
    h = h_ref[...]
    halo = jnp.where(i % tiles_per_seq == 0, jnp.zeros_like(hp_ref[...]), hp_ref[...])
    h_ext = jnp.concatenate([halo, h], axis=0)
    starts = np.cumsum((0,) + FFN_COLS)
    groups = [slice(int(starts[g]), int(starts[g + 1])) for g in range(len(FFN_COLS))]

    def up(cs):
        gate_cs = slice(D_FF + cs.start, D_FF + cs.stop)
        return _dot(h_ext, wu_ref[:, cs]), _dot(h, wu_ref[:, gate_cs])

    acc = x1_ref[...]
    nxt = up(groups[0])
    for g, cs in enumerate(groups):
        a, b = nxt
        if g + 1 < len(groups):
            nxt = up(groups[g + 1])
        conv = (a[FFN_HALO:, :] * cw_ref[2:3, cs] + a[FFN_HALO - 1:FFN_HALO - 1 + tm, :] * cw_ref[1:2, cs]
                + a[FFN_HALO - 2:FFN_HALO - 2 + tm, :] * cw_ref[0:1, cs] + cb_ref[:, cs])
        gelu = 0.5 * conv * (1.0 + lax.erf(conv * (2.0 ** -0.5)))
        acc = acc + _dot((gelu * b).astype(BF16), wd_ref[cs, :])
    o_ref[...] = acc


def _ffn(h2, x1, wu, cw, cb, wd, tm, seq):
    t = h2.shape[0]
    row = lambda n: pl.BlockSpec((tm, n), lambda i: (i, 0))
    halo = pl.BlockSpec((FFN_HALO, D_MODEL), lambda i: (jnp.maximum(i * (tm // FFN_HALO) - 1, 0), 0))
    return pl.pallas_call(
        functools.partial(_ffn_kernel, tiles_per_seq=seq // tm),
        grid=(t // tm,),
        in_specs=[row(D_MODEL), halo, row(D_MODEL), _const_spec(wu.shape),
                  _const_spec(cw.shape), _const_spec(cb.shape), _const_spec(wd.shape)],
        out_specs=row(D_MODEL),
        out_shape=jax.ShapeDtypeStruct((t, D_MODEL), F32),
        compiler_params=_params("parallel"),
        name="ffn",
    )(h2, h2, x1, wu, cw, cb, wd)


def _pad_lora_cols(a):
    z = lambda n: jnp.zeros(a.shape[:-1] + (n,), a.dtype)
    o1, o2 = DECAY_LORA, DECAY_LORA + AAA_LORA
    return jnp.concatenate([a[..., :o1], z(LANES - DECAY_LORA), a[..., o1:o2], z(LANES - AAA_LORA),
                            a[..., o2:], z(2 * LANES - GATE_LORA)], axis=-1)


def _pad_rows(a, n):
    return jnp.concatenate([a, jnp.zeros((n - a.shape[0],) + a.shape[1:], a.dtype)], axis=0)


def kernel(x, positions, norm1_g, w_in, rwkv_mu, w_decay_up, decay_bias, w_aaa_up, aaa_bias, w_gate_up, rwkv_k_k, rwkv_k_a, rwkv_r_k, rwkv_ln_g, rwkv_ln_b, q_norm_g, k_norm_g, w_branch_a, w_branch_b, w_out, norm2_g, w_ffn_up, ffn_conv_w, ffn_conv_b, w_ffn_down):
    batch, seq, _ = x.shape
    depth = norm1_g.shape[0]
    assert seq % MOBA_BLOCK == 0 and seq % 512 == 0
    t = batch * seq
    row = lambda a: a.reshape(1, -1)
    c3 = 3 * WIDTH
    rwkv_in = c3 + DECAY_LORA + AAA_LORA + GATE_LORA
    hsum = _rwkv_masks()[1]

    x2 = x.reshape(t, D_MODEL)
    for l in range(depth):
        wi = w_in[l]
        wr = wi[:, :c3].astype(BF16)
        wl = _pad_lora_cols(wi[:, c3:rwkv_in]).astype(BF16)
        wm = wi[:, rwkv_in:rwkv_in + c3].astype(BF16)
        wg = wi[:, rwkv_in + c3:].astype(BF16)
        zr, zl, zm, gates = _inproj(x2, row(norm1_g[l]), wr, wl, wm, wg, tm=512)

        consts = [row(rwkv_mu[l][:c3]), _pad_lora_cols(row(rwkv_mu[l][c3:])),
                  _pad_rows(w_decay_up[l], LANES).astype(BF16), row(decay_bias[l]),
                  _pad_rows(w_aaa_up[l], LANES).astype(BF16), row(aaa_bias[l]),
                  _pad_rows(w_gate_up[l], 2 * LANES).astype(BF16),
                  row(rwkv_k_k[l]), row(rwkv_k_a[l]), row(rwkv_r_k[l]),
                  row(rwkv_ln_g[l]), row(rwkv_ln_b[l])]
        ya = _rwkv(zr, zl, consts, batch, seq).reshape(t, WIDTH)

        tile8 = lambda a: row(jnp.tile(a, N_HEADS))
        qa, ka, va = _moba_prep(zm, positions, tile8(q_norm_g[l]), tile8(k_norm_g[l]), hsum, batch, seq)
        yb = _moba_attn(qa, ka, va, batch, seq).reshape(t, WIDTH)

        x1, h2 = _merge(x2, ya, yb, gates, w_branch_a[l].astype(BF16), w_branch_b[l].astype(BF16),
                        w_out[l].astype(BF16), row(norm2_g[l]), tm=512)

        x2 = _ffn(h2, x1, w_ffn_up[l].astype(BF16), ffn_conv_w[l],
                  row(ffn_conv_b[l]), w_ffn_down[l].astype(BF16), tm=512, seq=seq)
    return x2.reshape(batch, seq, D_MODEL)
```

```python
import functools

import numpy as np
import jax
import jax.numpy as jnp
from jax import lax
from jax.experimental import pallas as pl
from jax.experimental.pallas import tpu as pltpu

F32 = jnp.float32
BF16 = jnp.bfloat16

D_MODEL = 1024
HEAD_DIM = 64
N_HEADS = 8
WIDTH = N_HEADS * HEAD_DIM
DECAY_LORA = 64
AAA_LORA = 64
GATE_LORA = 160
LORA_PAD = 512
MOBA_BLOCK = 256
MOBA_TOPK = 3
ROT_DIM = HEAD_DIM // 4
ROPE_THETA = 500000.0
D_FF = 2816
NORM_EPS = 1e-6
GN_EPS = 64e-5
LANES = 128
CHUNK = 64
NEG_BIG = -(2.0 ** 100)
VMEM_LIMIT = 56 * 1024 * 1024


def _sigmoid(x):
    return 1.0 / (1.0 + jnp.exp(-x))


def _dot(a, b, precision=None):
    return jnp.dot(a, b, preferred_element_type=F32, precision=precision)


def _dot_nt(a, b, precision=None):
    return lax.dot_general(a, b, (((1,), (1,)), ((), ())), preferred_element_type=F32,
                           precision=precision)


def _head_sums(x, bd_ones):
    g = bd_ones.shape[0]
    return jnp.concatenate([_dot(x[:, j:j + g].astype(BF16), bd_ones) for j in range(0, x.shape[1], g)],
                           axis=1)


def _dot_tn(a, b, precision=None):
    return lax.dot_general(a, b, (((0,), (0,)), ((), ())), preferred_element_type=F32,
                           precision=precision)


def _params(*sem):
    return pltpu.CompilerParams(dimension_semantics=sem, vmem_limit_bytes=VMEM_LIMIT)


def _const_spec(shape):
    nd = len(shape)
    return pl.BlockSpec(shape, lambda *_: (0,) * nd, pipeline_mode=pl.Buffered(1))


def _inproj_kernel(x_ref, g_ref, wr_ref, wl_ref, wm_ref, wg_ref,
                   zr_ref, zl_ref, zm_ref, gate_ref):
    half = x_ref.shape[0] // 2
    for rows in (slice(0, half), slice(half, 2 * half)):
        x = x_ref[rows, :]
        ms = jnp.mean(x * x, axis=-1, keepdims=True)
        h = (x * lax.rsqrt(ms + NORM_EPS) * g_ref[...]).astype(BF16)
        zr_ref[rows, :] = _dot(h, wr_ref[...])
        zl_ref[rows, :] = _dot(h, wl_ref[...])
        zm_ref[rows, :] = _dot(h, wm_ref[...])
        gate_ref[rows, :] = _sigmoid(_dot(h, wg_ref[...])).astype(BF16)


def _inproj(x2, g, wr, wl, wm, wg, tm):
    t = x2.shape[0]
    row = lambda n: pl.BlockSpec((tm, n), lambda i: (i, 0))
    return pl.pallas_call(
        _inproj_kernel,
        grid=(t // tm,),
        in_specs=[row(D_MODEL), _const_spec(g.shape), _const_spec(wr.shape), _const_spec(wl.shape),
                  _const_spec(wm.shape), _const_spec(wg.shape)],
        out_specs=[row(3 * WIDTH), row(LORA_PAD), row(3 * WIDTH), row(2 * D_MODEL)],
        out_shape=[jax.ShapeDtypeStruct((t, 3 * WIDTH), F32),
                   jax.ShapeDtypeStruct((t, LORA_PAD), F32),
                   jax.ShapeDtypeStruct((t, 3 * WIDTH), F32),
                   jax.ShapeDtypeStruct((t, 2 * D_MODEL), BF16)],
        compiler_params=_params("parallel"),
        name="inproj",
    )(x2, g, wr, wl, wm, wg)


RW_ROWS = 256
RW_SEQS = 4
RW_GROUP = 256


def _rwkv_kernel(zr_ref, zrp_ref, zl_ref, zlp_ref, mur_ref, mul_ref, wd_ref, db_ref, wa_ref,
                 ab_ref, wg_ref, kk_ref, ka_ref, rk_ref, lng_ref, lnb_ref, cum_ref, bd_ref,
                 o_ref, state_ref):
    c = pl.program_id(1)
    n_seq, TS = zr_ref.shape[0], zr_ref.shape[1]
    TL = n_seq * TS
    L = CHUNK
    G = RW_GROUP

    @pl.when(c == 0)
    def _():
        state_ref[...] = jnp.zeros_like(state_ref)

    def token_shift(z_ref, zp_ref, mu_ref):
        z = z_ref[...].reshape(TL, z_ref.shape[2])
        row = lax.broadcasted_iota(jnp.int32, z.shape, 0)
        zs = pltpu.roll(z, 1, axis=0)
        for s in range(n_seq):
            zs = jnp.where(row == s * TS, jnp.where(c == 0, 0.0, zp_ref[s, 7:8, :]), zs)
        return z + mu_ref[...] * (zs - z)

    zr = token_shift(zr_ref, zrp_ref, mur_ref)
    zl = token_shift(zl_ref, zlp_ref, mul_ref)
    r = zr[:, 0:WIDTH]
    k = zr[:, WIDTH:2 * WIDTH]
    v = zr[:, 2 * WIDTH:3 * WIDTH]
    xw = zl[:, 0:LANES]
    xa = zl[:, LANES:2 * LANES]
    xg = zl[:, 2 * LANES:4 * LANES]

    dd = db_ref[...] + _dot(jnp.tanh(xw).astype(BF16), wd_ref[...])
    w_log = -(jnp.maximum(-dd, 0.0) + jnp.log(1.0 + jnp.exp(-jnp.abs(dd)))) - 0.5
    lw = -jnp.exp(w_log)
    asig = _sigmoid(ab_ref[...] + _dot(xa.astype(BF16), wa_ref[...]))
    g = _dot(_sigmoid(xg).astype(BF16), wg_ref[...])
    bdm = bd_ref[...]
    kkf = k * kk_ref[...]
    kk = kkf * lax.rsqrt(jnp.maximum(_head_sums(kkf * kkf, bdm), 1e-24))
    kmod = k * (1.0 + (asig - 1.0) * ka_ref[...])
    bonus = _head_sums(r * kmod * rk_ref[...], bdm) * v
    b = kk * asig

    lw_hi = lw.astype(BF16)
    lw_lo = (lw - lw_hi.astype(F32)).astype(BF16)
    cum = cum_ref[...]
    cums = [_dot(cum, lw_hi[ci * L:(ci + 1) * L]) + _dot(cum, lw_lo[ci * L:(ci + 1) * L]) for ci in range(TL // L)]
    cw = jnp.concatenate([cs[0:L] for cs in cums], axis=0)
    cw_end = jnp.concatenate([cs[L:2 * L] for cs in cums], axis=0)
    e_neg = jnp.exp(-cw)
    e_end = jnp.exp(cw_end - cw)
    a_til = (-kk * jnp.exp(cw - lw)).astype(BF16)
    r_til = (r * jnp.exp(cw)).astype(BF16)
    b_til = (b * e_neg).astype(BF16)
    k_til = (kmod * e_neg).astype(BF16)
    b_hat = b * e_end
    k_hat = kmod * e_end
    w_end = jnp.exp(cw_end)
    vb = v.astype(BF16)

    def bd(x):
        return jnp.concatenate([x.astype(BF16)] * (G // L), axis=0) * bdm

    lane_lo = lax.broadcasted_iota(jnp.int32, (L, LANES), 1) < HEAD_DIM

    def head_t(x):
        t = jnp.concatenate([x, x], axis=0).T
        return jnp.concatenate([jnp.where(lane_lo, t[2 * j * L:(2 * j + 1) * L], t[(2 * j + 1) * L:(2 * j + 2) * L])
                                for j in range(G // LANES)], axis=1)

    row2 = lax.broadcasted_iota(jnp.int32, (2 * L, G), 0)
    lane_t = lax.broadcasted_iota(jnp.int32, (2 * L, G), 1) % L
    tri2 = lane_t < jnp.where(row2 < L, row2, row2 - L + 1)
    eye = (lax.broadcasted_iota(jnp.int32, (L, G), 1) % L
           == lax.broadcasted_iota(jnp.int32, (L, G), 0)).astype(F32)

    n_chunks, n_groups = TL // L, WIDTH // G
    pairs = [(ci, gi) for ci in range(n_chunks) for gi in range(n_groups)]

    def blk(x, cg):
        ci, gi = cg
        return x[ci * L:(ci + 1) * L, gi * G:(gi + 1) * G]

    p = {cg: jnp.concatenate([blk(a_til, cg), blk(r_til, cg)], axis=0) for cg in pairs}
    ab_rb = {cg: jnp.where(tri2, _dot_nt(p[cg], bd(blk(b_til, cg))), 0.0) for cg in pairs}
    ak_rk = {cg: jnp.where(tri2, _dot_nt(p[cg], bd(blk(k_til, cg))), 0.0) for cg in pairs}
    tinv = {cg: eye + ab_rb[cg][0:L] for cg in pairs}
    pw = {cg: _dot(ab_rb[cg][0:L].astype(BF16), bd(ab_rb[cg][0:L])) for cg in pairs}
    for _ in range(int(np.log2(L)) - 2):
        both = {cg: _dot(jnp.concatenate([pw[cg], tinv[cg]], axis=0).astype(BF16), bd(pw[cg])) for cg in pairs}
        pw = {cg: both[cg][0:L] for cg in pairs}
        tinv = {cg: tinv[cg] + both[cg][L:2 * L] for cg in pairs}
    tinv = {cg: (tinv[cg] + _dot(tinv[cg].astype(BF16), bd(pw[cg]))).astype(BF16) for cg in pairs}
    bd_v = {cg: bd(blk(vb, cg)) for cg in pairs}
    kv = {cg: _dot(ak_rk[cg].astype(BF16), bd_v[cg]) for cg in pairs}
    kt_v = {cg: _dot(head_t(blk(k_hat, cg)).astype(BF16), bd_v[cg]) for cg in pairs}
    arb_bt = {cg: jnp.concatenate([ab_rb[cg][L:2 * L], head_t(blk(b_hat, cg))], axis=0).astype(BF16)
              for cg in pairs}
    w_row = {cg: head_t(blk(w_end, cg)) for cg in pairs}

    chains = [(s, gi) for s in range(n_seq) for gi in range(n_groups)]
    seq_chunks = TS // L
    st = {sg: state_ref[sg[0] * n_groups + sg[1]] for sg in chains}
    y_blk = {}
    for ci in range(seq_chunks):
        cg = {sg: (sg[0] * seq_chunks + ci, sg[1]) for sg in chains}
        ps = {sg: _dot(p[cg[sg]], bd(st[sg])) for sg in chains}
        u = {sg: _dot(tinv[cg[sg]], bd(ps[sg][0:L] + kv[cg[sg]][0:L])) for sg in chains}
        yu = {sg: _dot(arb_bt[cg[sg]], bd(u[sg])) for sg in chains}
        st = {sg: w_row[cg[sg]] * st[sg] + yu[sg][L:2 * L] + kt_v[cg[sg]] for sg in chains}
        for sg in chains:
            y_blk[cg[sg]] = ps[sg][L:2 * L] + kv[cg[sg]][L:2 * L] + yu[sg][0:L]
    for sg in chains:
        state_ref[sg[0] * n_groups + sg[1]] = st[sg]
    y = jnp.concatenate([jnp.concatenate([y_blk[(ci, gi)] for gi in range(n_groups)], axis=1)
                         for ci in range(n_chunks)], axis=0)

    inv_n = 1.0 / HEAD_DIM
    mu = _head_sums(y, bdm) * inv_n
    yc = y - mu
    var = _head_sums(yc * yc, bdm) * inv_n
    yn = yc * lax.rsqrt(var + GN_EPS) * lng_ref[...] + lnb_ref[...]
    o_ref[...] = ((yn + bonus) * g).astype(o_ref.dtype).reshape(o_ref.shape)


def _rwkv_masks():
    t = np.arange(CHUNK)
    tri = t[:, None] >= t[None, :]
    cum = np.concatenate([tri, np.ones_like(tri)], axis=0)
    g = np.arange(RW_GROUP)
    bd = (g[:, None] // HEAD_DIM) == (g[None, :] // HEAD_DIM)
    return [jnp.asarray(m, BF16) for m in (cum, bd)]


def _rwkv(zr, zl, consts, batch, seq):
    TL = RW_ROWS
    nt = seq // TL
    ns = RW_SEQS if batch % RW_SEQS == 0 else 1
    cur = lambda n: pl.BlockSpec((ns, TL, n), lambda b, c: (b, c, 0))
    prv = lambda n: pl.BlockSpec((ns, 8, n), lambda b, c: (b, jnp.maximum(c * (TL // 8) - 1, 0), 0))
    zr3 = zr.reshape(batch, seq, 3 * WIDTH)
    zl3 = zl.reshape(batch, seq, LORA_PAD)
    consts = list(consts) + _rwkv_masks()
    return pl.pallas_call(
        _rwkv_kernel,
        grid=(batch // ns, nt),
        in_specs=[cur(3 * WIDTH), prv(3 * WIDTH), cur(LORA_PAD), prv(LORA_PAD)]
                 + [_const_spec(a.shape) for a in consts],
        out_specs=pl.BlockSpec((ns, TL, WIDTH), lambda b, c: (b, c, 0)),
        out_shape=jax.ShapeDtypeStruct((batch, seq, WIDTH), BF16),
        scratch_shapes=[pltpu.VMEM((ns * (WIDTH // RW_GROUP), HEAD_DIM, RW_GROUP), F32)],
        compiler_params=_params("parallel", "arbitrary"),
        name="rwkv",
    )(zr3, zr3, zl3, zl3, *consts)


QK_SCALE_LOG2 = HEAD_DIM ** -0.5 * float(np.log2(np.e))
AUX_BLOCKS = LANES // N_HEADS
HALF_HEADS = N_HEADS // 2
VT_ROWS = 80


def _moba_prep_kernel(zm_ref, pos_ref, invf_ref, ropep_ref, qg_ref, kg_ref, hsum_ref,
                      q_out, k_out, v_out, km_ref, *, n_blocks):
    i = pl.program_id(1)
    TB = zm_ref.shape[0]

    @pl.when(i == 0)
    def _():
        km_ref[...] = jnp.zeros_like(km_ref)

    zm = zm_ref[...]
    ang = invf_ref[...] * pos_ref[...].astype(F32)
    cs = jnp.concatenate([jnp.cos(ang), jnp.sin(ang)], axis=0)
    cs_hi = cs.astype(BF16)
    cs_lo = (cs - cs_hi.astype(F32)).astype(BF16)
    tab = _dot_tn(jnp.concatenate([cs_hi, cs_lo], axis=0), ropep_ref[...])
    lane = lax.broadcasted_iota(jnp.int32, (TB, LANES), 1)
    rep = WIDTH // LANES
    half = ROT_DIM // 2
    cos = jnp.concatenate([tab[:, 0:LANES] + jnp.where(lane % HEAD_DIM >= ROT_DIM, 1.0, 0.0)] * rep, axis=-1)
    sin_lo = jnp.concatenate([tab[:, LANES:2 * LANES]] * rep, axis=-1)
    sin_hi = jnp.concatenate([tab[:, 2 * LANES:3 * LANES]] * rep, axis=-1)
    hsum = hsum_ref[...]

    def norm_rope(t, gain):
        ms = _head_sums(t * t, hsum) * (1.0 / HEAD_DIM)
        t = t * lax.rsqrt(ms + NORM_EPS) * gain
        return (t * cos + pltpu.roll(t, WIDTH - half, axis=1) * sin_lo
                + pltpu.roll(t, half, axis=1) * sin_hi)

    q = norm_rope(zm[:, 0:WIDTH], qg_ref[...])
    k = norm_rope(zm[:, WIDTH:2 * WIDTH], kg_ref[...])
    v = zm[:, 2 * WIDTH:3 * WIDTH]

    km = km_ref[...]
    km_hi, q_hi = km.astype(BF16), q.astype(BF16)
    km_lo, q_lo = (km - km_hi.astype(F32)).astype(BF16), (q - q_hi.astype(F32)).astype(BF16)
    gate2 = _dot_nt(jnp.concatenate([km_hi, km_lo], axis=0), q_hi)
    gate = (gate2[0:LANES] + gate2[LANES:2 * LANES] + _dot_nt(km_hi, q_lo)).reshape(N_HEADS, AUX_BLOCKS, TB)
    n_idx = lax.broadcasted_iota(jnp.int32, gate.shape, 1)
    gsel = jnp.where(n_idx < i, gate, -jnp.inf)
    picked = n_idx < 0
    for _ in range(MOBA_TOPK):
        m = jnp.max(gsel, axis=1, keepdims=True)
        idx = jnp.min(jnp.where(gsel == m, n_idx, AUX_BLOCKS), axis=1, keepdims=True)
        pick = n_idx == idx
        picked = picked | (pick & (m > -jnp.inf))
        gsel = jnp.where(pick, -jnp.inf, gsel)
    keep = picked | (n_idx == i) | (n_idx >= n_blocks)
    bias = jnp.where(keep, 0.0, NEG_BIG).reshape(LANES, TB).T
    bias_by_half = [pltpu.roll(bias, HEAD_DIM, axis=1), bias]

    rowk = lax.broadcasted_iota(jnp.int32, km.shape, 0)
    lanek = lax.broadcasted_iota(jnp.int32, km.shape, 1)
    mine = (rowk % AUX_BLOCKS == i) & (lanek // HEAD_DIM == rowk // AUX_BLOCKS)
    km_ref[...] = jnp.where(mine, jnp.mean(k, axis=0, keepdims=True), km)

    ones_rows = (lax.broadcasted_iota(jnp.int32, (VT_ROWS - HEAD_DIM, TB), 0) == 0).astype(F32)
    for pr in range(N_HEADS // 2):
        ps = slice(pr * LANES, (pr + 1) * LANES)
        qp, kp = q[:, ps] * QK_SCALE_LOG2, k[:, ps]
        vt = v[:, ps].T
        for odd in range(2):
            h = 2 * pr + odd
            is_data = (lane >= HEAD_DIM) if odd else (lane < HEAD_DIM)
            aux_base = 0 if odd else HEAD_DIM
            upper = h // HALF_HEADS
            own_lane = aux_base + (h % HALF_HEADS) * AUX_BLOCKS + i
            q_out[h] = jnp.where(is_data, qp, bias_by_half[1 - upper if odd else upper]).astype(BF16)
            k_out[h] = jnp.where(is_data, kp, jnp.where(lane == own_lane, 1.0, 0.0)).astype(BF16)
            v_out[h] = jnp.concatenate([vt[odd * HEAD_DIM:(odd + 1) * HEAD_DIM], ones_rows], axis=0).astype(BF16)


def _rope_tables():
    half = ROT_DIM // 2
    invf = ROPE_THETA ** (-np.arange(half, dtype=np.float64) / half)
    lane = np.arange(LANES) % HEAD_DIM
    f = np.arange(half)[:, None]
    p_cos = ((lane[None, :] < ROT_DIM) & (lane[None, :] % half == f)).astype(np.float64)
    p_lo = -(lane[None, :] == f).astype(np.float64)
    p_hi = (lane[None, :] == f + half).astype(np.float64)
    z = np.zeros_like(p_cos)
    cos_rows = np.concatenate([p_cos, z, z], axis=1)
    sin_rows = np.concatenate([z, p_lo, p_hi], axis=1)
    expand = np.concatenate([cos_rows, sin_rows, cos_rows, sin_rows], axis=0)
    return (jnp.asarray(np.broadcast_to(invf[:, None], (half, MOBA_BLOCK)), F32), jnp.asarray(expand, BF16))


def _moba_prep(zm, positions, qg, kg, hsum, batch, seq):
    TB = MOBA_BLOCK
    nb = seq // TB
    assert nb <= AUX_BLOCKS
    zm3 = zm.reshape(batch, seq, 3 * WIDTH)
    pos4 = positions.reshape(batch, nb, 1, TB)
    invf, expand = _rope_tables()
    aug = pl.BlockSpec((None, N_HEADS, TB, LANES), lambda b, i: (b, 0, i, 0))
    aug_shape = jax.ShapeDtypeStruct((batch, N_HEADS, seq, LANES), BF16)
    return pl.pallas_call(
        functools.partial(_moba_prep_kernel, n_blocks=nb),
        grid=(batch, nb),
        in_specs=[pl.BlockSpec((None, TB, 3 * WIDTH), lambda b, i: (b, i, 0)),
                  pl.BlockSpec((None, None, 1, TB), lambda b, i: (b, i, 0, 0)),
                  _const_spec(invf.shape), _const_spec(expand.shape), _const_spec(qg.shape),
                  _const_spec(kg.shape), _const_spec(hsum.shape)],
        out_specs=[aug, aug, pl.BlockSpec((None, N_HEADS, None, VT_ROWS, TB), lambda b, i: (b, 0, i, 0, 0))],
        out_shape=[aug_shape, aug_shape, jax.ShapeDtypeStruct((batch, N_HEADS, nb, VT_ROWS, TB), BF16)],
        scratch_shapes=[pltpu.VMEM((LANES, WIDTH), F32)],
        compiler_params=_params("parallel", "arbitrary"),
        name="moba_prep",
    )(zm3, pos4, invf, expand, qg, kg, hsum)


def _moba_attn_kernel(q_ref, k_ref, v_ref, o_ref, acc_ref, m_ref, s_ref, p_ref, a_ref, mx_ref):
    i = pl.program_id(2)
    TB = q_ref.shape[1]
    HP = q_ref.shape[0]
    n_steps = i + 1

    def block_of(tau):
        return jnp.where(tau == 0, i, jnp.minimum(tau - 1, jnp.maximum(i - 1, 0)))

    def qk(tau, par, masked=False):
        koff = pl.multiple_of(block_of(tau) * TB, TB)
        for hh in range(HP):
            s = _dot_nt(k_ref[hh, pl.ds(koff, TB), :], q_ref[hh])
            if masked:
                key = lax.broadcasted_iota(jnp.int32, (TB, TB), 0)
                qry = lax.broadcasted_iota(jnp.int32, (TB, TB), 1)
                s = jnp.where(key <= qry, s, -1e30)
            s_ref[par, hh] = s
            mx_ref[par, hh] = jnp.max(s, axis=0, keepdims=True)

    def softmax(par):
        for hh in range(HP):
            m_old = m_ref[hh]
            m_new = jnp.maximum(m_old, mx_ref[par, hh])
            m_ref[hh] = m_new
            a_ref[par, hh] = jnp.exp2(m_old - m_new)
            p_ref[par, hh] = jnp.exp2(s_ref[par, hh] - m_new).astype(BF16)

    def pv(tau, par):
        blk = block_of(tau)
        for hh in range(HP):
            acc_ref[hh] = acc_ref[hh] * a_ref[par, hh] + _dot(v_ref[hh, blk], p_ref[par, hh])

    for hh in range(HP):
        m_ref[hh] = jnp.full((1, TB), -1e30, F32)
        acc_ref[hh] = jnp.zeros((VT_ROWS, TB), F32)
    qk(0, 0, masked=True)
    qk(1, 1)
    softmax(0)

    def slot_pair(pp, carry):
        t = 2 * pp
        qk(t + 2, 0)
        pv(t, 0)
        softmax(1)
        qk(t + 3, 1)
        pv(t + 1, 1)
        softmax(0)
        return carry

    full_pairs = jnp.maximum(n_steps - 2, 0) // 2
    lax.fori_loop(0, full_pairs, slot_pair, 0)
    t0 = 2 * full_pairs
    left = n_steps - t0

    @pl.when(left == 1)
    def _():
        pv(t0, 0)

    @pl.when(left == 2)
    def _():
        pv(t0, 0)
        softmax(1)
        pv(t0 + 1, 1)

    @pl.when(left == 3)
    def _():
        qk(t0 + 2, 0)
        pv(t0, 0)
        softmax(1)
        pv(t0 + 1, 1)
        softmax(0)
        pv(t0 + 2, 0)

    outs = []
    for hh in range(HP):
        acc = acc_ref[hh]
        out_t = jnp.concatenate([acc / acc[HEAD_DIM:HEAD_DIM + 1, :], jnp.zeros((LANES - VT_ROWS, TB), F32)],
                                axis=0)
        outs.append(out_t.T)
    lane = lax.broadcasted_iota(jnp.int32, (TB, LANES), 1)
    for pr in range(HP // 2):
        o_ref[:, pr * LANES:(pr + 1) * LANES] = jnp.where(
            lane < HEAD_DIM, outs[2 * pr], pltpu.roll(outs[2 * pr + 1], HEAD_DIM, axis=1)).astype(o_ref.dtype)


ATTN_HEADS = 8


def _moba_attn(qa, ka, va, batch, seq):
    TB = MOBA_BLOCK
    nb = seq // TB
    HP = ATTN_HEADS
    return pl.pallas_call(
        _moba_attn_kernel,
        grid=(batch, N_HEADS // HP, nb),
        in_specs=[pl.BlockSpec((None, HP, TB, LANES), lambda b, p, i: (b, p, i, 0)),
                  pl.BlockSpec((None, HP, seq, LANES), lambda b, p, i: (b, p, 0, 0)),
                  pl.BlockSpec((None, HP, nb, VT_ROWS, TB), lambda b, p, i: (b, p, 0, 0, 0))],
        out_specs=pl.BlockSpec((None, TB, HP * HEAD_DIM), lambda b, p, i: (b, i, p)),
        out_shape=jax.ShapeDtypeStruct((batch, seq, WIDTH), BF16),
        scratch_shapes=[pltpu.VMEM((HP, VT_ROWS, TB), F32), pltpu.VMEM((HP, 1, TB), F32),
                        pltpu.VMEM((2, HP, TB, TB), F32), pltpu.VMEM((2, HP, TB, TB), BF16),
                        pltpu.VMEM((2, HP, 1, TB), F32), pltpu.VMEM((2, HP, 1, TB), F32)],
        compiler_params=_params("parallel", "parallel", "arbitrary"),
        name="moba_attn",
    )(qa, ka, va)


def _merge_kernel(x_ref, ya_ref, yb_ref, gate_ref, wa_ref, wb_ref, wo_ref, g2_ref, x1_ref, h2_ref):
    half = x_ref.shape[0] // 2
    halves = (slice(0, half), slice(half, 2 * half))
    ua = [_dot(ya_ref[rows, :], wa_ref[...]) for rows in halves]
    ub = [_dot(yb_ref[rows, :], wb_ref[...]) for rows in halves]
    for hi, rows in enumerate(halves):
        gate = gate_ref[rows, :].astype(F32)
        mix = (gate[:, 0:D_MODEL] * ua[hi] + gate[:, D_MODEL:] * ub[hi]).astype(BF16)
        x1 = x_ref[rows, :] + _dot(mix, wo_ref[...])
        x1_ref[rows, :] = x1
        ms = jnp.mean(x1 * x1, axis=-1, keepdims=True)
        h2_ref[rows, :] = (x1 * lax.rsqrt(ms + NORM_EPS) * g2_ref[...]).astype(BF16)


def _merge(x2, ya, yb, gates, wa, wb, wo, g2, tm):
    t = x2.shape[0]
    row = lambda n: pl.BlockSpec((tm, n), lambda i: (i, 0))
    return pl.pallas_call(
        _merge_kernel,
        grid=(t // tm,),
        in_specs=[row(D_MODEL), row(WIDTH), row(WIDTH), row(2 * D_MODEL), _const_spec(wa.shape),
                  _const_spec(wb.shape), _const_spec(wo.shape), _const_spec(g2.shape)],
        out_specs=[row(D_MODEL), row(D_MODEL)],
        out_shape=[jax.ShapeDtypeStruct((t, D_MODEL), F32), jax.ShapeDtypeStruct((t, D_MODEL), BF16)],
        compiler_params=_params("parallel"),
        name="merge",
    )(x2, ya, yb, gates, wa, wb, wo, g2)


FFN_HALO = 16
FFN_COLS = (768, 768, 768, 512)


def _ffn_kernel(h_ref, hp_ref, x1_ref, wu_ref, cw_ref, cb_ref, wd_ref, o_ref, *, tiles_per_seq):
    i = pl.program_id(0)
    tm = h_ref.shape[0]
    h = h_ref[...]
    halo = jnp.where(i % tiles_per_seq == 0, jnp.zeros_like(hp_ref[...]), hp_ref[...])
    h_ext = jnp.concatenate([halo, h], axis=0)
    starts = np.cumsum((0,) + FFN_COLS)
    groups = [slice(int(starts[g]), int(starts[g + 1])) for g in range(len(FFN_COLS))]

    def up(cs):
        gate_cs = slice(D_FF + cs.start, D_FF + cs.stop)
        return _dot(h_ext, wu_ref[:, cs]), _dot(h, wu_ref[:, gate_cs])

    acc = x1_ref[...]
    nxt = up(groups[0])
    for g, cs in enumerate(groups):
        a, b = nxt
        if g + 1 < len(groups):
            nxt = up(groups[g + 1])
        conv = (a[FFN_HALO:, :] * cw_ref[2:3, cs] + a[FFN_HALO - 1:FFN_HALO - 1 + tm, :] * cw_ref[1:2, cs]
                + a[FFN_HALO - 2:FFN_HALO - 2 + tm, :] * cw_ref[0:1, cs] + cb_ref[:, cs])
        gelu = 0.5 * conv * (1.0 + lax.erf(conv * (2.0 ** -0.5)))
        acc = acc + _dot((gelu * b).astype(BF16), wd_ref[cs, :])
    o_ref[...] = acc


def _ffn(h2, x1, wu, cw, cb, wd, tm, seq):
    t = h2.shape[0]
    row = lambda n: pl.BlockSpec((tm, n), lambda i: (i, 0))
    halo = pl.BlockSpec((FFN_HALO, D_MODEL), lambda i: (jnp.maximum(i * (tm // FFN_HALO) - 1, 0), 0))
    return pl.pallas_call(
        functools.partial(_ffn_kernel, tiles_per_seq=seq // tm),
        grid=(t // tm,),
        in_specs=[row(D_MODEL), halo, row(D_MODEL), _const_spec(wu.shape),
                  _const_spec(cw.shape), _const_spec(cb.shape), _const_spec(wd.shape)],
        out_specs=row(D_MODEL),
        out_shape=jax.ShapeDtypeStruct((t, D_MODEL), F32),
        compiler_params=_params("parallel"),
        name="ffn",
    )(h2, h2, x1, wu, cw, cb, wd)


def _pad_lora_cols(a):
    z = lambda n: jnp.zeros(a.shape[:-1] + (n,), a.dtype)
    o1, o2 = DECAY_LORA, DECAY_LORA + AAA_LORA
    return jnp.concatenate([a[..., :o1], z(LANES - DECAY_LORA), a[..., o1:o2], z(LANES - AAA_LORA),
                            a[..., o2:], z(2 * LANES - GATE_LORA)], axis=-1)


def _pad_rows(a, n):
    return jnp.concatenate([a, jnp.zeros((n - a.shape[0],) + a.shape[1:], a.dtype)], axis=0)


def kernel(x, positions, norm1_g, w_in, rwkv_mu, w_decay_up, decay_bias, w_aaa_up, aaa_bias, w_gate_up, rwkv_k_k, rwkv_k_a, rwkv_r_k, rwkv_ln_g, rwkv_ln_b, q_norm_g, k_norm_g, w_branch_a, w_branch_b, w_out, norm2_g, w_ffn_up, ffn_conv_w, ffn_conv_b, w_ffn_down):
    batch, seq, _ = x.shape
    depth = norm1_g.shape[0]
    assert seq % MOBA_BLOCK == 0 and seq % 512 == 0
    t = batch * seq
    row = lambda a: a.reshape(1, -1)
    c3 = 3 * WIDTH
    rwkv_in = c3 + DECAY_LORA + AAA_LORA + GATE_LORA
    hsum = _rwkv_masks()[1]

    x2 = x.reshape(t, D_MODEL)
    for l in range(depth):
        wi = w_in[l]
        wr = wi[:, :c3].astype(BF16)
        wl = _pad_lora_cols(wi[:, c3:rwkv_in]).astype(BF16)
        wm = wi[:, rwkv_in:rwkv_in + c3].astype(BF16)
        wg = wi[:, rwkv_in + c3:].astype(BF16)
        zr, zl, zm, gates = _inproj(x2, row(norm1_g[l]), wr, wl, wm, wg, tm=512)

        consts = [row(rwkv_mu[l][:c3]), _pad_lora_cols(row(rwkv_mu[l][c3:])),
                  _pad_rows(w_decay_up[l], LANES).astype(BF16), row(decay_bias[l]),
                  _pad_rows(w_aaa_up[l], LANES).astype(BF16), row(aaa_bias[l]),
                  _pad_rows(w_gate_up[l], 2 * LANES).astype(BF16),
                  row(rwkv_k_k[l]), row(rwkv_k_a[l]), row(rwkv_r_k[l]),
                  row(rwkv_ln_g[l]), row(rwkv_ln_b[l])]
        ya = _rwkv(zr, zl, consts, batch, seq).reshape(t, WIDTH)

        tile8 = lambda a: row(jnp.tile(a, N_HEADS))
        qa, ka, va = _moba_prep(zm, positions, tile8(q_norm_g[l]), tile8(k_norm_g[l]), hsum, batch, seq)
        yb = _moba_attn(qa, ka, va, batch, seq).reshape(t, WIDTH)

        x1, h2 = _merge(x2, ya, yb, gates, w_branch_a[l].astype(BF16), w_branch_b[l].astype(BF16),
                        w_out[l].astype(BF16), row(norm2_g[l]), tm=512)

        x2 = _ffn(h2, x1, w_ffn_up[l].astype(BF16), ffn_conv_w[l],
                  row(ffn_conv_b[l]), w_ffn_down[l].astype(BF16), tm=512, seq=seq)
    return x2.reshape(batch, seq, D_MODEL)
```
